```python
import math
import jax, jax.numpy as jnp
from jax import lax
import numpy as np

D_MODEL = 2048
BATCH = 4
SEQ = 2048
DEPTH = 4
DEC_BATCH = 128
DEC_SEQ = 4
PAST_LEN = 8192
PAGE_SIZE = 128

SSM_W = 1024
SSM_GROUP = 16
SSM_GROUPS = SSM_W // SSM_GROUP
SSM_STATE = 64
N_HEADS = 16
Q_LORA = 512
KV_LORA = 512
QK_NOPE = 128
QK_ROPE = 64
V_DIM = 128
ROPE_BASE = 10000.0
Q_BLOCK = 128
ATTN_SCALE = (QK_NOPE + QK_ROPE) ** -0.5
NEG = -1e30
POOL_W = 1024
POOL_WINDOWS = (2, 4, 8, 16)
POOL_GROUP = POOL_W // len(POOL_WINDOWS)
POOL_BUF = max(POOL_WINDOWS) - 1
N_BRANCH = 3
D_FF = (((8 * D_MODEL + 2) // 3 + 255) // 256) * 256
EPS = 1e-6
OFF_SSM = 0
OFF_Q = OFF_SSM + SSM_W
OFF_KV = OFF_Q + Q_LORA
OFF_KR = OFF_KV + KV_LORA
OFF_POOL = OFF_KR + QK_ROPE
OFF_GATE = OFF_POOL + POOL_W
IN_W = OFF_GATE + N_BRANCH * D_MODEL

kernel_name = "hybrid_s5_mla_pool_decoder_step"


def rms_norm(x, g):
    xf = x.astype(jnp.float32)
    y = xf * lax.rsqrt(jnp.mean(xf * xf, axis=-1, keepdims=True) + EPS)
    return (y * g.astype(jnp.float32)).astype(x.dtype)


def rope_angles(pos):
    inv = 1.0 / (ROPE_BASE ** (jnp.arange(0, QK_ROPE, 2, dtype=jnp.float32) / QK_ROPE))
    ang = pos.astype(jnp.float32)[:, None] * inv[None, :]
    return jnp.cos(ang), jnp.sin(ang)


def apply_rope(x, cos, sin):
    xf = x.astype(jnp.float32)
    x1, x2 = xf[..., : QK_ROPE // 2], xf[..., QK_ROPE // 2:]
    c, s = cos[None, :, None, :], sin[None, :, None, :]
    return jnp.concatenate([x1 * c - x2 * s, x2 * c + x1 * s], axis=-1).astype(x.dtype)


def s5_discretize(a_re, a_im, log_dt, b_re, b_im):
    f32 = jnp.float32
    dt = jnp.exp(log_dt.astype(f32))[:, None]
    a_re, a_im = a_re.astype(f32), a_im.astype(f32)
    mag = jnp.exp(dt * a_re)
    ab_re, ab_im = mag * jnp.cos(dt * a_im), mag * jnp.sin(dt * a_im)
    den = a_re * a_re + a_im * a_im
    n_re, n_im = ab_re - 1.0, ab_im
    c_re = (n_re * a_re + n_im * a_im) / den
    c_im = (n_im * a_re - n_re * a_im) / den
    b_re, b_im = b_re.astype(f32), b_im.astype(f32)
    bb_re = c_re[..., None] * b_re - c_im[..., None] * b_im
    bb_im = c_re[..., None] * b_im + c_im[..., None] * b_re
    return ab_re, ab_im, bb_re, bb_im


def s5_combine(e1, e2):
    a1r, a1i, b1r, b1i = e1
    a2r, a2i, b2r, b2i = e2
    return (a2r * a1r - a2i * a1i, a2r * a1i + a2i * a1r,
            a2r * b1r - a2i * b1i + b2r, a2r * b1i + a2i * b1r + b2i)


def s5_branch(u, s0_re, s0_im, a_re, a_im, log_dt, b_re, b_im, c_re, c_im, d, w_glu, b_glu):
    f32 = jnp.float32
    bsz, L, _ = u.shape
    ab_re, ab_im, bb_re, bb_im = s5_discretize(a_re, a_im, log_dt, b_re, b_im)
    uf = u.astype(f32).reshape(bsz, L, SSM_GROUPS, SSM_GROUP)
    x_re = jnp.einsum('blgi,gpi->lbgp', uf, bb_re)
    x_im = jnp.einsum('blgi,gpi->lbgp', uf, bb_im)
    s0_re, s0_im = s0_re.astype(f32), s0_im.astype(f32)
    x_re = x_re.at[0].add(ab_re * s0_re - ab_im * s0_im)
    x_im = x_im.at[0].add(ab_re * s0_im + ab_im * s0_re)
    a_re_l = jnp.broadcast_to(ab_re, (L, 1, SSM_GROUPS, SSM_STATE))
    a_im_l = jnp.broadcast_to(ab_im, (L, 1, SSM_GROUPS, SSM_STATE))
    _, _, s_re, s_im = lax.associative_scan(s5_combine, (a_re_l, a_im_l, x_re, x_im), axis=0)
    y = (jnp.einsum('lbgp,gip->blgi', s_re, c_re.astype(f32))
         - jnp.einsum('lbgp,gip->blgi', s_im, c_im.astype(f32)))
    y = y.reshape(bsz, L, SSM_W) + d.astype(f32) * u.astype(f32)
    g = jax.nn.gelu(y).astype(u.dtype)
    out = g * jax.nn.sigmoid(g @ w_glu + b_glu)
    return out, s_re[-1], s_im[-1]


def pool_branch(u, buf, start_pos, pool_w, pool_scale):
    bsz, L, C = u.shape
    xc = jnp.concatenate([buf.astype(u.dtype), u], axis=1).astype(jnp.float32)
    csum = jnp.concatenate([jnp.zeros((bsz, 1, C), jnp.float32), jnp.cumsum(xc, axis=1)], axis=1)
    pos = start_pos + jnp.arange(L)
    outs = []
    for gi, w in enumerate(POOL_WINDOWS):
        lo_c, hi_c = gi * POOL_GROUP, (gi + 1) * POOL_GROUP
        hi = csum[:, POOL_BUF + 1: POOL_BUF + 1 + L, lo_c:hi_c]
        lo = csum[:, POOL_BUF + 1 - w: POOL_BUF + 1 - w + L, lo_c:hi_c]
        cnt = jnp.minimum(pos + 1, w).astype(jnp.float32)[None, :, None]
        outs.append((hi - lo) / cnt - xc[:, POOL_BUF:, lo_c:hi_c])
    m = jnp.stack(outs, axis=2).astype(u.dtype)
    y = jnp.einsum('blgi,gio->blgo', m, pool_w).reshape(bsz, L, POOL_W)
    new_buf = xc[:, -POOL_BUF:, :].astype(u.dtype)
    return y * pool_scale, new_buf


def mla_prompt_attention(q_nope, q_rope, c_kv, k_rope, w_kv_b):
    bsz, L = q_nope.shape[:2]
    kv = (c_kv @ w_kv_b).reshape(bsz, L, N_HEADS, QK_NOPE + V_DIM)
    k_nope, v = kv[..., :QK_NOPE], kv[..., QK_NOPE:]
    nblk = L // Q_BLOCK
    qn = q_nope.reshape(bsz, nblk, Q_BLOCK, N_HEADS, QK_NOPE).transpose(1, 0, 2, 3, 4)
    qr = q_rope.reshape(bsz, nblk, Q_BLOCK, N_HEADS, QK_ROPE).transpose(1, 0, 2, 3, 4)
    kpos = jnp.arange(L)

    def block(args):
        qn_b, qr_b, i = args
        s = (jnp.einsum('bqhd,bkhd->bhqk', qn_b, k_nope)
             + jnp.einsum('bqhd,bkd->bhqk', qr_b, k_rope)).astype(jnp.float32) * ATTN_SCALE
        qpos = i * Q_BLOCK + jnp.arange(Q_BLOCK)
        s = jnp.where(kpos[None, :] <= qpos[:, None], s, NEG)
        p = jax.nn.softmax(s, axis=-1).astype(v.dtype)
        return jnp.einsum('bhqk,bkhd->bqhd', p, v)

    o = lax.map(block, (qn, qr, jnp.arange(nblk)))
    return o.transpose(1, 0, 2, 3, 4).reshape(bsz, L, N_HEADS * V_DIM)


def mla_sample_attention(q_nope, q_rope, c_kv, k_rope, w_kv_b, cache_lat, cache_rope, layer, page_table):
    f32 = jnp.float32
    bsz, S = q_nope.shape[:2]
    w = w_kv_b.reshape(KV_LORA, N_HEADS, QK_NOPE + V_DIM)
    w_uk, w_uv = w[..., :QK_NOPE], w[..., QK_NOPE:]
    q_lat = jnp.einsum('bqhd,chd->bqhc', q_nope, w_uk)

    def scores(ckv, kr):
        return (jnp.einsum('bqhc,btc->bhqt', q_lat, ckv.astype(q_lat.dtype))
                + jnp.einsum('bqhd,btd->bhqt', q_rope, kr.astype(q_rope.dtype))).astype(f32) * ATTN_SCALE

    def accumulate(carry, s, ckv):
        m, lsum, acc = carry
        m_new = jnp.maximum(m, jnp.max(s, axis=-1))
        corr = jnp.exp(m - m_new)
        p = jnp.exp(s - m_new[..., None])
        acc = acc * corr[..., None] + jnp.einsum('bhqt,btc->bhqc', p, ckv.astype(f32))
        return (m_new, lsum * corr + jnp.sum(p, axis=-1), acc)

    def page_step(carry, pages):
        ckv = cache_lat[layer, pages]
        kr = cache_rope[layer, pages]
        return accumulate(carry, scores(ckv, kr), ckv), None

    init = (jnp.full((bsz, N_HEADS, S), NEG, f32), jnp.zeros((bsz, N_HEADS, S), f32),
            jnp.zeros((bsz, N_HEADS, S, KV_LORA), f32))
    carry, _ = lax.scan(page_step, init, page_table.T)
    causal = jnp.arange(S)[None, :] <= jnp.arange(S)[:, None]
    s_self = jnp.where(causal, scores(c_kv, k_rope), NEG)
    _, lsum, acc = accumulate(carry, s_self, c_kv)
    o_lat = (acc / lsum[..., None]).astype(q_nope.dtype)
    return jnp.einsum('bhqc,chd->bqhd', o_lat, w_uv).reshape(bsz, S, N_HEADS * V_DIM)


def mixer_inputs(h, cos, sin, w_in_l, q_norm_l, w_q_b_l, kv_norm_l):
    bsz, L, _ = h.shape
    z = h @ w_in_l
    u_ssm = z[..., OFF_SSM:OFF_Q]
    c_q = rms_norm(z[..., OFF_Q:OFF_KV], q_norm_l)
    c_kv = rms_norm(z[..., OFF_KV:OFF_KR], kv_norm_l)
    k_rope = apply_rope(z[..., OFF_KR:OFF_POOL][:, :, None, :], cos, sin)[:, :, 0, :]
    u_pool = z[..., OFF_POOL:OFF_GATE]
    gates = jax.nn.sigmoid(z[..., OFF_GATE:].reshape(bsz, L, N_BRANCH, D_MODEL))
    q = (c_q @ w_q_b_l).reshape(bsz, L, N_HEADS, QK_NOPE + QK_ROPE)
    q_nope = q[..., :QK_NOPE]
    q_rope = apply_rope(q[..., QK_NOPE:], cos, sin)
    return u_ssm, q_nope, q_rope, c_kv, k_rope, u_pool, gates


def trunk_layer(x, start_pos, cos, sin, s_re, s_im, pool_buf, attend,
                norm_mix_l, norm_ffn_l, w_in_l, ssm_params, q_norm_l, w_q_b_l, kv_norm_l, w_kv_b_l,
                pool_w_l, pool_scale_l, w_br_ssm_l, w_br_mla_l, w_br_pool_l, w_out_l,
                w_gate_l, w_up_l, w_down_l):
    bsz, L, _ = x.shape
    h = rms_norm(x, norm_mix_l)
    u_ssm, q_nope, q_rope, c_kv, k_rope, u_pool, gates = mixer_inputs(
        h, cos, sin, w_in_l, q_norm_l, w_q_b_l, kv_norm_l)
    o_ssm, new_re, new_im = s5_branch(u_ssm, s_re, s_im, *ssm_params)
    o_mla = attend(q_nope, q_rope, c_kv, k_rope, w_kv_b_l)
    o_pool, new_buf = pool_branch(u_pool, pool_buf, start_pos, pool_w_l, pool_scale_l)
    merged = (gates[:, :, 0] * (o_ssm @ w_br_ssm_l)
              + gates[:, :, 1] * (o_mla @ w_br_mla_l)
              + gates[:, :, 2] * (o_pool @ w_br_pool_l))
    x = x + merged @ w_out_l
    h2 = rms_norm(x, norm_ffn_l)
    x = x + (jax.nn.silu(h2 @ w_gate_l) * (h2 @ w_up_l)) @ w_down_l
    return x, c_kv, k_rope, new_re, new_im, new_buf


def setup_inputs(seed: int = 0) -> dict:
    key = jax.random.key(seed)
    ks = iter(jax.random.split(key, 48))
    f32 = jnp.float32

    def nrm(shape, scale):
        return jax.random.normal(next(ks), shape, f32) * scale

    n_pages = PAST_LEN // PAGE_SIZE
    n_phys = (DEC_BATCH * n_pages * 5) // 4
    page_table = jax.random.permutation(next(ks), n_phys)[: DEC_BATCH * n_pages]
    page_table = page_table.reshape(DEC_BATCH, n_pages).astype(jnp.int32)
    n_idx = jnp.arange(SSM_STATE, dtype=f32)
    gs = (DEPTH, SSM_GROUPS, SSM_STATE)
    return {
        "x_prompt": nrm((BATCH, SEQ, D_MODEL), 1.0),
        "x_sample": nrm((DEC_BATCH, DEC_SEQ, D_MODEL), 1.0),
        "cache_kv_latent": nrm((DEPTH, n_phys, PAGE_SIZE, KV_LORA), 1.0),
        "cache_k_rope": nrm((DEPTH, n_phys, PAGE_SIZE, QK_ROPE), 1.0),
        "state_ssm_re": nrm((DEPTH, DEC_BATCH, SSM_GROUPS, SSM_STATE), 0.5),
        "state_ssm_im": nrm((DEPTH, DEC_BATCH, SSM_GROUPS, SSM_STATE), 0.5),
        "state_pool_buf": nrm((DEPTH, DEC_BATCH, POOL_BUF, POOL_W), 1.0),
        "page_table": page_table,
        "norm_mix": 1.0 + nrm((DEPTH, D_MODEL), 0.02),
        "norm_ffn": 1.0 + nrm((DEPTH, D_MODEL), 0.02),
        "norm_final": 1.0 + nrm((D_MODEL,), 0.02),
        "w_in": nrm((DEPTH, D_MODEL, IN_W), D_MODEL ** -0.5),
        "ssm_a_re": -0.5 + nrm(gs, 0.01),
        "ssm_a_im": math.pi * n_idx + nrm(gs, 0.01),
        "ssm_log_dt": jax.random.uniform(next(ks), (DEPTH, SSM_GROUPS), f32, math.log(1e-3), math.log(1e-1)),
        "ssm_b_re": nrm((DEPTH, SSM_GROUPS, SSM_STATE, SSM_GROUP), (2 * SSM_GROUP) ** -0.5),
        "ssm_b_im": nrm((DEPTH, SSM_GROUPS, SSM_STATE, SSM_GROUP), (2 * SSM_GROUP) ** -0.5),
        "ssm_c_re": nrm((DEPTH, SSM_GROUPS, SSM_GROUP, SSM_STATE), (2 * SSM_STATE) ** -0.5),
        "ssm_c_im": nrm((DEPTH, SSM_GROUPS, SSM_GROUP, SSM_STATE), (2 * SSM_STATE) ** -0.5),
        "ssm_d": nrm((DEPTH, SSM_W), 0.5),
        "ssm_w_glu": nrm((DEPTH, SSM_W, SSM_W), SSM_W ** -0.5),
        "ssm_b_glu": nrm((DEPTH, SSM_W), 0.01),
        "mla_q_norm": 1.0 + nrm((DEPTH, Q_LORA), 0.02),
        "mla_w_q_b": nrm((DEPTH, Q_LORA, N_HEADS * (QK_NOPE + QK_ROPE)), Q_LORA ** -0.5),
        "mla_kv_norm": 1.0 + nrm((DEPTH, KV_LORA), 0.02),
        "mla_w_kv_b": nrm((DEPTH, KV_LORA, N_HEADS * (QK_NOPE + V_DIM)), KV_LORA ** -0.5),
        "pool_w": nrm((DEPTH, len(POOL_WINDOWS), POOL_GROUP, POOL_GROUP), POOL_GROUP ** -0.5),
        "pool_scale": 1.0 + nrm((DEPTH, POOL_W), 0.02),
        "w_br_ssm": nrm((DEPTH, SSM_W, D_MODEL), SSM_W ** -0.5),
        "w_br_mla": nrm((DEPTH, N_HEADS * V_DIM, D_MODEL), (N_HEADS * V_DIM) ** -0.5),
        "w_br_pool": nrm((DEPTH, POOL_W, D_MODEL), POOL_W ** -0.5),
        "w_out": nrm((DEPTH, D_MODEL, D_MODEL), D_MODEL ** -0.5),
        "ffn_w_gate": nrm((DEPTH, D_MODEL, D_FF), D_MODEL ** -0.5),
        "ffn_w_up": nrm((DEPTH, D_MODEL, D_FF), D_MODEL ** -0.5),
        "ffn_w_down": nrm((DEPTH, D_FF, D_MODEL), D_FF ** -0.5),
    }


def reference(x_prompt, x_sample, cache_kv_latent, cache_k_rope, state_ssm_re, state_ssm_im, state_pool_buf,
              page_table, norm_mix, norm_ffn, norm_final, w_in,
              ssm_a_re, ssm_a_im, ssm_log_dt, ssm_b_re, ssm_b_im, ssm_c_re, ssm_c_im, ssm_d, ssm_w_glu, ssm_b_glu,
              mla_q_norm, mla_w_q_b, mla_kv_norm, mla_w_kv_b, pool_w, pool_scale,
              w_br_ssm, w_br_mla, w_br_pool, w_out, ffn_w_gate, ffn_w_up, ffn_w_down):
    cos_p, sin_p = rope_angles(jnp.arange(SEQ))
    cos_s, sin_s = rope_angles(PAST_LEN + jnp.arange(DEC_SEQ))
    zero_state = jnp.zeros((BATCH, SSM_GROUPS, SSM_STATE), jnp.float32)
    zero_buf = jnp.zeros((BATCH, POOL_BUF, POOL_W), x_prompt.dtype)
    xp, xs = x_prompt, x_sample
    p_lat, p_kr, p_re, p_im, p_buf = [], [], [], [], []
    s_lat, s_kr, s_re, s_im, s_buf = [], [], [], [], []
    for l in range(DEPTH):
        ssm_params = (ssm_a_re[l], ssm_a_im[l], ssm_log_dt[l], ssm_b_re[l], ssm_b_im[l],
                      ssm_c_re[l], ssm_c_im[l], ssm_d[l], ssm_w_glu[l], ssm_b_glu[l])
        weights = (norm_mix[l], norm_ffn[l], w_in[l], ssm_params, mla_q_norm[l], mla_w_q_b[l],
                   mla_kv_norm[l], mla_w_kv_b[l], pool_w[l], pool_scale[l],
                   w_br_ssm[l], w_br_mla[l], w_br_pool[l], w_out[l],
                   ffn_w_gate[l], ffn_w_up[l], ffn_w_down[l])
        xp, c_kv, k_r, n_re, n_im, n_buf = trunk_layer(
            xp, 0, cos_p, sin_p, zero_state, zero_state, zero_buf, mla_prompt_attention, *weights)
        p_lat.append(c_kv); p_kr.append(k_r); p_re.append(n_re); p_im.append(n_im); p_buf.append(n_buf)
        attend_s = (lambda qn, qr, ck, kr, wkv, _l=l: mla_sample_attention(
            qn, qr, ck, kr, wkv, cache_kv_latent, cache_k_rope, _l, page_table))
        xs, c_kv, k_r, n_re, n_im, n_buf = trunk_layer(
            xs, PAST_LEN, cos_s, sin_s, state_ssm_re[l], state_ssm_im[l], state_pool_buf[l], attend_s, *weights)
        s_lat.append(c_kv); s_kr.append(k_r); s_re.append(n_re); s_im.append(n_im); s_buf.append(n_buf)
    y_prompt = rms_norm(xp, norm_final)
    y_sample = rms_norm(xs, norm_final)
    return (y_prompt, y_sample,
            jnp.stack(p_lat), jnp.stack(p_kr), jnp.stack(p_re), jnp.stack(p_im), jnp.stack(p_buf),
            jnp.stack(s_lat), jnp.stack(s_kr), jnp.stack(s_re), jnp.stack(s_im), jnp.stack(s_buf))
```

```python
import functools
import math

import jax
import jax.numpy as jnp
from jax import lax
from jax.experimental import pallas as pl
from jax.experimental.pallas import tpu as pltpu

F32 = jnp.float32
BF16 = jnp.bfloat16

D_MODEL = 2048
BATCH = 4
SEQ = 2048
DEPTH = 4
DEC_BATCH = 128
DEC_SEQ = 4
PAST_LEN = 8192
PAGE_SIZE = 128
N_PAGES = PAST_LEN // PAGE_SIZE
SSM_W = 1024
SSM_GROUP = 16
SSM_GROUPS = SSM_W // SSM_GROUP
SSM_STATE = 64
N_HEADS = 16
Q_LORA = 512
KV_LORA = 512
QK_NOPE = 128
QK_ROPE = 64
V_DIM = 128
ROPE_BASE = 10000.0
ATTN_SCALE = (QK_NOPE + QK_ROPE) ** -0.5
NEG = -1e30
POOL_W = 1024
POOL_WINDOWS = (2, 4, 8, 16)
POOL_GROUP = POOL_W // len(POOL_WINDOWS)
POOL_BUF = max(POOL_WINDOWS) - 1
POOL_HALO = 16
N_BRANCH = 3
D_FF = (((8 * D_MODEL + 2) // 3 + 255) // 256) * 256
EPS = 1e-6
OFF_SSM = 0
OFF_Q = OFF_SSM + SSM_W
OFF_KV = OFF_Q + Q_LORA
OFF_KR = OFF_KV + KV_LORA
OFF_POOL = OFF_KR + QK_ROPE
OFF_GATE = OFF_POOL + POOL_W

N_PROMPT = BATCH * SEQ
N_SAMPLE = DEC_BATCH * DEC_SEQ
N_TOK = N_PROMPT + N_SAMPLE
HEAD_PAD = 256
LANES = 128
S5_CHUNK = 16
S5_GROUPS_PER_STEP = 8
PAGES_PER_STEP = 16
POOL_TILE = 256
TM = 512
VMEM_LIMIT = 56 * 1024 * 1024


def _params(*sem):
    return pltpu.CompilerParams(dimension_semantics=sem, vmem_limit_bytes=VMEM_LIMIT)


def _sigmoid(x):
    return 1.0 / (1.0 + jnp.exp(-x))


def _rmsnorm_kernel(x_ref, g_ref, o_ref):
    x = x_ref[...]
    ms = jnp.mean(x * x, axis=-1, keepdims=True)
    o_ref[...] = (x * lax.rsqrt(ms + EPS) * g_ref[...]).astype(o_ref.dtype)


def _rmsnorm(x, g, out_dtype):
    m, d = x.shape
    return pl.pallas_call(
        _rmsnorm_kernel,
        grid=(m // TM,),
        in_specs=[pl.BlockSpec((TM, d), lambda i: (i, 0)), pl.BlockSpec((1, d), lambda i: (0, 0))],
        out_specs=pl.BlockSpec((TM, d), lambda i: (i, 0)),
        out_shape=jax.ShapeDtypeStruct((m, d), out_dtype),
        compiler_params=_params("parallel"),
        name="rmsnorm",
    )(x, g.reshape(1, d))


def _linear_kernel(*refs, act, has_res):
    x_ref, w_ref = refs[0], refs[1]
    o_ref = refs[-1]
    z = jnp.dot(x_ref[...], w_ref[...], preferred_element_type=F32)
    if act == "sigmoid":
        z = _sigmoid(z)
    if has_res:
        z = refs[2][...] + z
    o_ref[...] = z.astype(o_ref.dtype)


def _linear(x, w, out_dtype, *, tn, act=None, residual=None, name="linear"):
    m, k = x.shape
    n = w.shape[1]
    tm = min(TM, m)
    in_specs = [pl.BlockSpec((tm, k), lambda i, j: (i, 0)), pl.BlockSpec((k, tn), lambda i, j: (0, j))]
    args = [x, w]
    if residual is not None:
        in_specs.append(pl.BlockSpec((tm, tn), lambda i, j: (i, j)))
        args.append(residual)
    return pl.pallas_call(
        functools.partial(_linear_kernel, act=act, has_res=residual is not None),
        grid=(m // tm, n // tn),
        in_specs=in_specs,
        out_specs=pl.BlockSpec((tm, tn), lambda i, j: (i, j)),
        out_shape=jax.ShapeDtypeStruct((m, n), out_dtype),
        compiler_params=_params("parallel", "parallel"),
        name=name,
    )(*args)


def _latent_kernel(x_ref, w_ref, g_ref, o32_ref, o16_ref):
    z = jnp.dot(x_ref[...], w_ref[...], preferred_element_type=F32)
    ms = jnp.mean(z * z, axis=-1, keepdims=True)
    y = z * lax.rsqrt(ms + EPS) * g_ref[0]
    o32_ref[...] = y
    o16_ref[...] = y.astype(BF16)


def _latent_proj(h, w, gains):
    m, k = h.shape
    n = w.shape[1]
    tn = Q_LORA
    return pl.pallas_call(
        _latent_kernel,
        grid=(m // TM, n // tn),
        in_specs=[pl.BlockSpec((TM, k), lambda i, j: (i, 0)),
                  pl.BlockSpec((k, tn), lambda i, j: (0, j)),
                  pl.BlockSpec((1, 1, tn), lambda i, j: (j, 0, 0))],
        out_specs=[pl.BlockSpec((TM, tn), lambda i, j: (i, j)), pl.BlockSpec((TM, tn), lambda i, j: (i, j))],
        out_shape=[jax.ShapeDtypeStruct((m, n), F32), jax.ShapeDtypeStruct((m, n), BF16)],
        compiler_params=_params("parallel", "parallel"),
        name="latent_proj",
    )(h, w, gains)


def _rope_tile(z, cos_t, sin_t):
    return z * cos_t + pltpu.roll(z, 64, axis=1) * sin_t


def _krope_kernel(x_ref, w_ref, cos_ref, sin_ref, o32_ref, o16_ref):
    z = jnp.dot(x_ref[...], w_ref[...], preferred_element_type=F32)
    y = _rope_tile(z, cos_ref[...], sin_ref[...])
    o32_ref[...] = y
    o16_ref[...] = y.astype(BF16)


def _krope_proj(h, w, cos_t, sin_t):
    m, k = h.shape
    row = lambda i: (i, 0)
    return pl.pallas_call(
        _krope_kernel,
        grid=(m // TM,),
        in_specs=[pl.BlockSpec((TM, k), row), pl.BlockSpec((k, LANES), lambda i: (0, 0)),
                  pl.BlockSpec((TM, LANES), row), pl.BlockSpec((TM, LANES), row)],
        out_specs=[pl.BlockSpec((TM, LANES), row), pl.BlockSpec((TM, LANES), row)],
        out_shape=[jax.ShapeDtypeStruct((m, LANES), F32), jax.ShapeDtypeStruct((m, LANES), BF16)],
        compiler_params=_params("parallel"),
        name="krope_proj",
    )(h, w, cos_t, sin_t)


def _q_kernel(x_ref, w_ref, cos_ref, sin_ref, o_ref, *, heads):
    z = jnp.dot(x_ref[...], w_ref[...], preferred_element_type=F32) * ATTN_SCALE
    cos_t, sin_t = cos_ref[...], sin_ref[...]
    for h in range(heads):
        base = h * HEAD_PAD
        o_ref[:, base:base + QK_NOPE] = z[:, base:base + QK_NOPE].astype(BF16)
        o_ref[:, base + QK_NOPE:base + HEAD_PAD] = _rope_tile(
            z[:, base + QK_NOPE:base + HEAD_PAD], cos_t, sin_t).astype(BF16)


def _q_proj(c_q, w, cos_t, sin_t):
    m, k = c_q.shape
    n = w.shape[1]
    heads = 2
    tn = heads * HEAD_PAD
    return pl.pallas_call(
        functools.partial(_q_kernel, heads=heads),
        grid=(m // TM, n // tn),
        in_specs=[pl.BlockSpec((TM, k), lambda i, j: (i, 0)), pl.BlockSpec((k, tn), lambda i, j: (0, j)),
                  pl.BlockSpec((TM, LANES), lambda i, j: (i, 0)), pl.BlockSpec((TM, LANES), lambda i, j: (i, 0))],
        out_specs=pl.BlockSpec((TM, tn), lambda i, j: (i, j)),
        out_shape=jax.ShapeDtypeStruct((m, n), BF16),
        compiler_params=_params("parallel", "parallel"),
        name="q_proj",
    )(c_q, w, cos_t, sin_t)


def _kv_kernel(x_ref, wk_ref, wv_ref, kr_ref, k_ref, v_ref, *, heads):
    x = x_ref[...]
    zk = jnp.dot(x, wk_ref[...], preferred_element_type=F32).astype(BF16)
    v_ref[...] = jnp.dot(x, wv_ref[...], preferred_element_type=F32).astype(BF16)
    kr = kr_ref[...]
    for h in range(heads):
        k_ref[:, h * HEAD_PAD:h * HEAD_PAD + QK_NOPE] = zk[:, h * QK_NOPE:(h + 1) * QK_NOPE]
        k_ref[:, h * HEAD_PAD + QK_NOPE:(h + 1) * HEAD_PAD] = kr


def _kv_proj(lat16, wk, wv, kr16, m):
    k = KV_LORA
    heads = 2
    return pl.pallas_call(
        functools.partial(_kv_kernel, heads=heads),
        grid=(m // TM, N_HEADS // heads),
        in_specs=[pl.BlockSpec((TM, k), lambda i, j: (i, 1)),
                  pl.BlockSpec((k, heads * QK_NOPE), lambda i, j: (0, j)),
                  pl.BlockSpec((k, heads * V_DIM), lambda i, j: (0, j)),
                  pl.BlockSpec((TM, LANES), lambda i, j: (i, 0))],
        out_specs=[pl.BlockSpec((TM, heads * HEAD_PAD), lambda i, j: (i, j)),
                   pl.BlockSpec((TM, heads * V_DIM), lambda i, j: (i, j))],
        out_shape=[jax.ShapeDtypeStruct((m, N_HEADS * HEAD_PAD), BF16),
                   jax.ShapeDtypeStruct((m, N_HEADS * V_DIM), BF16)],
        compiler_params=_params("parallel", "parallel"),
        name="kv_proj",
    )(lat16, wk, wv, kr16)


FLASH_T = 512


def _flash_kernel(q_ref, k_ref, v_ref, o_ref, m_sc, l_sc, acc_sc):
    qi, kj = pl.program_id(2), pl.program_id(3)

    @pl.when(kj == 0)
    def _():
        m_sc[...] = jnp.full_like(m_sc, NEG)
        l_sc[...] = jnp.zeros_like(l_sc)
        acc_sc[...] = jnp.zeros_like(acc_sc)

    @pl.when(kj <= qi)
    def _():
        s = lax.dot_general(q_ref[...], k_ref[...], (((1,), (1,)), ((), ())), preferred_element_type=F32)
        row = lax.broadcasted_iota(jnp.int32, s.shape, 0)
        col = lax.broadcasted_iota(jnp.int32, s.shape, 1)
        s = jnp.where(jnp.logical_or(kj < qi, col <= row), s, NEG)
        m_prev = m_sc[...]
        m_new = jnp.maximum(m_prev, jnp.max(s, axis=-1, keepdims=True))
        corr = jnp.exp(m_prev - m_new)
        p = jnp.exp(s - m_new)
        l_sc[...] = corr * l_sc[...] + jnp.sum(p, axis=-1, keepdims=True)
        acc_sc[...] = corr * acc_sc[...] + jnp.dot(p.astype(BF16), v_ref[...], preferred_element_type=F32)
        m_sc[...] = m_new

    @pl.when(kj == pl.num_programs(3) - 1)
    def _():
        o_ref[...] = (acc_sc[...] / l_sc[...]).astype(o_ref.dtype)


def _flash_attention(q_full, k_full, v, bsz, seq):
    nblk = seq // FLASH_T
    t = FLASH_T
    return pl.pallas_call(
        _flash_kernel,
        grid=(bsz, N_HEADS, nblk, nblk),
        in_specs=[pl.BlockSpec((t, HEAD_PAD), lambda b, h, i, j: (b * nblk + i, h)),
                  pl.BlockSpec((t, HEAD_PAD), lambda b, h, i, j: (b * nblk + jnp.minimum(i, j), h)),
                  pl.BlockSpec((t, V_DIM), lambda b, h, i, j: (b * nblk + jnp.minimum(i, j), h))],
        out_specs=pl.BlockSpec((t, V_DIM), lambda b, h, i, j: (b * nblk + i, h)),
        out_shape=jax.ShapeDtypeStruct((bsz * seq, N_HEADS * V_DIM), BF16),
        scratch_shapes=[pltpu.VMEM((t, 1), F32), pltpu.VMEM((t, 1), F32), pltpu.VMEM((t, V_DIM), F32)],
        compiler_params=_params("parallel", "parallel", "parallel", "arbitrary"),
        name="flash_attention",
    )(q_full, k_full, v)


def _headwise_kernel(x_ref, w_ref, o_ref):
    o_ref[...] = jnp.dot(x_ref[...], w_ref[0], preferred_element_type=F32).astype(o_ref.dtype)


def _headwise_linear(x, w, x_block_of_head, name):
    m = x.shape[0]
    heads, kh, nh = w.shape
    return pl.pallas_call(
        _headwise_kernel,
        grid=(heads,),
        in_specs=[pl.BlockSpec((m, kh), lambda h: (0, x_block_of_head(h))),
                  pl.BlockSpec((1, kh, nh), lambda h: (h, 0, 0))],
        out_specs=pl.BlockSpec((m, nh), lambda h: (0, h)),
        out_shape=jax.ShapeDtypeStruct((m, heads * nh), BF16),
        compiler_params=_params("parallel"),
        name=name,
    )(x, w)


def _paged_kernel(pt_ref, qlat_ref, qrope_ref, cself_ref, rself_ref, *refs, pages):
    lat_refs, rope_refs = refs[:pages], refs[pages:2 * pages]
    o_ref = refs[2 * pages]
    m_sc, l_sc, acc_sc, lat_sc, rope_sc = refs[2 * pages + 1:]
    j = pl.program_id(1)
    nt = (((1,), (1,)), ((), ()))

    @pl.when(j == 0)
    def _():
        m_sc[...] = jnp.full_like(m_sc, NEG)
        l_sc[...] = jnp.zeros_like(l_sc)
        acc_sc[...] = jnp.zeros_like(acc_sc)

    for p in range(pages):
        lat_sc[p * PAGE_SIZE:(p + 1) * PAGE_SIZE, :] = lat_refs[p][0, 0].astype(BF16)
        rope_sc[p * PAGE_SIZE:(p + 1) * PAGE_SIZE, :] = rope_refs[p][0, 0].astype(BF16)

    qlat, qrope = qlat_ref[0], qrope_ref[0]

    def accumulate(s, values):
        m_prev = m_sc[...]
        m_new = jnp.maximum(m_prev, jnp.max(s, axis=-1, keepdims=True))
        corr = jnp.exp(m_prev - m_new)
        p = jnp.exp(s - m_new)
        l_sc[...] = corr * l_sc[...] + jnp.sum(p, axis=-1, keepdims=True)
        acc_sc[...] = corr * acc_sc[...] + jnp.dot(p.astype(BF16), values, preferred_element_type=F32)
        m_sc[...] = m_new

    lat = lat_sc[...]
    s = (lax.dot_general(qlat, lat, nt, preferred_element_type=F32)
         + lax.dot_general(qrope, rope_sc[...], nt, preferred_element_type=F32))
    accumulate(s, lat)

    @pl.when(j == pl.num_programs(1) - 1)
    def _():
        cself = cself_ref[0]
        s_self = (lax.dot_general(qlat, cself, nt, preferred_element_type=F32)
                  + lax.dot_general(qrope, rself_ref[0], nt, preferred_element_type=F32))
        qpos = lax.broadcasted_iota(jnp.int32, s_self.shape, 0) // N_HEADS
        kpos = lax.broadcasted_iota(jnp.int32, s_self.shape, 1)
        accumulate(jnp.where(kpos <= qpos, s_self, NEG), cself)
        o_ref[0] = (acc_sc[...] / l_sc[...]).astype(o_ref.dtype)


def _paged_attention(layer, page_table_flat, q_lat, q_rope, c_self, r_self, cache_lat, cache_rope):
    pages = PAGES_PER_STEP
    rows = DEC_SEQ * N_HEADS
    self_rows = c_self.shape[1]
    per_b = lambda b, j, pt: (b, 0, 0)

    def page_map(p):
        return lambda b, j, pt: (layer, pt[b * N_PAGES + j * pages + p], 0, 0)

    in_specs = [pl.BlockSpec((1, rows, KV_LORA), per_b), pl.BlockSpec((1, rows, QK_ROPE), per_b),
                pl.BlockSpec((1, self_rows, KV_LORA), per_b), pl.BlockSpec((1, self_rows, QK_ROPE), per_b)]
    in_specs += [pl.BlockSpec((1, 1, PAGE_SIZE, KV_LORA), page_map(p)) for p in range(pages)]
    in_specs += [pl.BlockSpec((1, 1, PAGE_SIZE, QK_ROPE), page_map(p)) for p in range(pages)]
    return pl.pallas_call(
        functools.partial(_paged_kernel, pages=pages),
        grid_spec=pltpu.PrefetchScalarGridSpec(
            num_scalar_prefetch=1,
            grid=(DEC_BATCH, N_PAGES // pages),
            in_specs=in_specs,
            out_specs=pl.BlockSpec((1, rows, KV_LORA), per_b),
            scratch_shapes=[pltpu.VMEM((rows, 1), F32), pltpu.VMEM((rows, 1), F32),
                            pltpu.VMEM((rows, KV_LORA), F32),
                            pltpu.VMEM((pages * PAGE_SIZE, KV_LORA), BF16),
                            pltpu.VMEM((pages * PAGE_SIZE, QK_ROPE), BF16)]),
        out_shape=jax.ShapeDtypeStruct((DEC_BATCH, rows, KV_LORA), BF16),
        compiler_params=_params("parallel", "arbitrary"),
        name="paged_attention",
    )(page_table_flat, q_lat, q_rope, c_self, r_self, *([cache_lat] * pages), *([cache_rope] * pages))


def _cexp(k, dt, a_re, a_im):
    mag = jnp.exp(k * dt * a_re)
    ang = k * dt * a_im
    return mag * jnp.cos(ang), mag * jnp.sin(ang)


def _cmul(x_re, x_im, y_re, y_im):
    return x_re * y_re - x_im * y_im, x_re * y_im + x_im * y_re


def _s5_prep_kernel(arow_re, arow_im, dtrow, bt_re, bt_im, acol_re, acol_im, dtcol, ct_re, ct_im,
                    toep_ref, bst_re_ref, bst_im_ref, cst_re_ref, cst_im_ref, pw_re_ref, pw_im_ref, *, chunk):
    groups = arow_re.shape[0]
    ti = chunk * SSM_GROUP

    @pl.loop(0, groups)
    def _(g):
        ar, ai, dt = arow_re[g], arow_im[g], dtrow[g]
        ab_re, ab_im = _cexp(1.0, dt, ar, ai)
        den = ar * ar + ai * ai
        n_re, n_im = ab_re - 1.0, ab_im
        c_re = (n_re * ar + n_im * ai) / den
        c_im = (n_im * ar - n_re * ai) / den
        bb_re, bb_im = _cmul(c_re, c_im, bt_re[g], bt_im[g])
        s_row = (lax.broadcasted_iota(jnp.int32, (ti, SSM_STATE), 0) // SSM_GROUP).astype(F32)
        p_re, p_im = _cmul(*_cexp(-s_row, dt, ar, ai), bb_re, bb_im)
        h_re, h_im = _cmul(*_cexp(chunk - 1.0 - s_row, dt, ar, ai), bb_re, bb_im)
        bst_re_ref[g] = h_re
        bst_im_ref[g] = h_im

        acr, aci, dtc = acol_re[g], acol_im[g], dtcol[g]
        t_col = (lax.broadcasted_iota(jnp.int32, (SSM_STATE, ti), 1) // SSM_GROUP).astype(F32)
        q_re, q_im = _cmul(*_cexp(t_col, dtc, acr, aci), ct_re[g], ct_im[g])
        r_re, r_im = _cmul(*_cexp(t_col + 1.0, dtc, acr, aci), ct_re[g], ct_im[g])
        cst_re_ref[g] = r_re
        cst_im_ref[g] = -r_im

        toep = (jnp.dot(p_re, q_re, preferred_element_type=F32, precision=lax.Precision.HIGHEST)
                - jnp.dot(p_im, q_im, preferred_element_type=F32, precision=lax.Precision.HIGHEST))
        s_idx = lax.broadcasted_iota(jnp.int32, (ti, ti), 0) // SSM_GROUP
        t_idx = lax.broadcasted_iota(jnp.int32, (ti, ti), 1) // SSM_GROUP
        toep_ref[g] = jnp.where(t_idx >= s_idx, toep, 0.0).astype(BF16)

        k_pow = chunk * jnp.left_shift(1, lax.broadcasted_iota(jnp.int32, (8, SSM_STATE), 0)).astype(F32)
        pw_re, pw_im = _cexp(k_pow, dt, ar, ai)
        pw_re_ref[g] = pw_re
        pw_im_ref[g] = pw_im


def _s5_prep(a_re, a_im, log_dt, b_re, b_im, c_re, c_im, chunk):
    g, p, ti = SSM_GROUPS, SSM_STATE, chunk * SSM_GROUP
    gs = S5_GROUPS_PER_STEP
    dt = jnp.exp(log_dt.astype(F32))
    row = lambda x: jnp.broadcast_to(x.astype(F32).reshape(g, 1, p), (g, 1, p))
    col = lambda x: jnp.broadcast_to(x.astype(F32).reshape(g, p, 1), (g, p, ti))
    bt = lambda x: jnp.tile(jnp.swapaxes(x.astype(F32), 1, 2), (1, chunk, 1))
    ct = lambda x: jnp.tile(jnp.swapaxes(x.astype(F32), 1, 2), (1, 1, chunk))
    args = [row(a_re), row(a_im), jnp.broadcast_to(dt.reshape(g, 1, 1), (g, 1, p)), bt(b_re), bt(b_im),
            col(a_re), col(a_im), jnp.broadcast_to(dt.reshape(g, 1, 1), (g, p, ti)), ct(c_re), ct(c_im)]
    blk = lambda shape: pl.BlockSpec((gs,) + shape, lambda i: (i, 0, 0))
    in_shapes = [(1, p), (1, p), (1, p), (ti, p), (ti, p), (p, ti), (p, ti), (p, ti), (p, ti), (p, ti)]
    out_shapes = [(ti, ti), (ti, p), (ti, p), (p, ti), (p, ti), (8, p), (8, p)]
    out_dtypes = [BF16, F32, F32, F32, F32, F32, F32]
    toep, bst_re, bst_im, cst_re, cst_im, pw_re, pw_im = pl.pallas_call(
        functools.partial(_s5_prep_kernel, chunk=chunk),
        grid=(g // gs,),
        in_specs=[blk(s) for s in in_shapes],
        out_specs=[blk(s) for s in out_shapes],
        out_shape=[jax.ShapeDtypeStruct((g,) + s, d) for s, d in zip(out_shapes, out_dtypes)],
        compiler_params=_params("parallel"),
        name="s5_prep",
    )(*args)
    bst = jnp.concatenate([bst_re, bst_im], axis=-1).astype(BF16)
    cst = jnp.concatenate([cst_re, cst_im], axis=1).astype(BF16)
    pwa = jnp.concatenate([pw_re, pw_re], axis=-1)
    pwb = jnp.concatenate([-pw_im, pw_im], axis=-1)
    return toep, bst, cst, pwa, pwb


def _s5_core_kernel(a_ref, toep_ref, bst_ref, cst_ref, pwa_ref, pwb_ref, s0_ref, y_ref, sfin_ref, *scratch,
                    batch, n_chunks):
    groups = a_ref.shape[0]
    rows = batch * n_chunks
    pad = 64

    @pl.loop(0, groups)
    def _(g):
        a = a_ref[g]
        y = jnp.dot(a, toep_ref[g], preferred_element_type=F32)
        s_loc = jnp.dot(a, bst_ref[g], preferred_element_type=F32)
        if n_chunks == 1:
            s0 = s0_ref[g]
            s_prev = s0
            s_end = s0 * pwa_ref[g, 0:1, :] + pltpu.roll(s0, 64, axis=1) * pwb_ref[g, 0:1, :] + s_loc
            sfin_ref[g] = s_end
        else:
            buf = scratch[0]
            c_idx = lax.broadcasted_iota(jnp.int32, (rows, 2 * SSM_STATE), 0) % n_chunks
            buf[0:pad, :] = jnp.zeros((pad, 2 * SSM_STATE), F32)
            x = s_loc
            shift, k = 1, 0
            while shift < n_chunks:
                buf[pad:pad + rows, :] = x
                xs = buf[pad - shift:pad - shift + rows, :]
                inc = xs * pwa_ref[g, k:k + 1, :] + pltpu.roll(xs, 64, axis=1) * pwb_ref[g, k:k + 1, :]
                x = x + jnp.where(c_idx >= shift, inc, 0.0)
                shift, k = shift * 2, k + 1
            buf[pad:pad + rows, :] = x
            s_prev = jnp.where(c_idx >= 1, buf[pad - 1:pad - 1 + rows, :], 0.0)
            for b in range(batch):
                sfin_ref[g, b:b + 1, :] = x[(b + 1) * n_chunks - 1:(b + 1) * n_chunks, :]
        y_ref[g] = y + jnp.dot(s_prev.astype(BF16), cst_ref[g], preferred_element_type=F32)


def _s5_core(a, ops, s0, batch, n_chunks):
    toep, bst, cst, pwa, pwb = ops
    g, rows, ti = a.shape
    gs = S5_GROUPS_PER_STEP
    p2 = 2 * SSM_STATE
    assert n_chunks == 1 or n_chunks <= 128
    blk = lambda shape: pl.BlockSpec((gs,) + shape, lambda i: (i, 0, 0))
    scratch = [] if n_chunks == 1 else [pltpu.VMEM((64 + rows, p2), F32)]
    return pl.pallas_call(
        functools.partial(_s5_core_kernel, batch=batch, n_chunks=n_chunks),
        grid=(g // gs,),
        in_specs=[blk((rows, ti)), blk((ti, ti)), blk((ti, p2)), blk((p2, ti)), blk((8, p2)), blk((8, p2)),
                  blk((batch, p2))],
        out_specs=[blk((rows, ti)), blk((batch, p2))],
        out_shape=[jax.ShapeDtypeStruct((g, rows, ti), F32), jax.ShapeDtypeStruct((g, batch, p2), F32)],
        scratch_shapes=scratch,
        compiler_params=_params("parallel"),
        name="s5_core",
    )(a, toep, bst, cst, pwa, pwb, s0)


def _glu_kernel(y_ref, u_ref, d_ref, w_ref, b_ref, o_ref):
    x = y_ref[...] + d_ref[...] * u_ref[...]
    g = 0.5 * x * (1.0 + jnp.tanh(math.sqrt(2.0 / math.pi) * (x + 0.044715 * (x * x * x))))
    z = jnp.dot(g.astype(BF16), w_ref[...], preferred_element_type=F32) + b_ref[...]
    o_ref[...] = (g * _sigmoid(z)).astype(o_ref.dtype)


def _glu(y, u, d, w, b):
    m, n = y.shape
    row, fixed = (lambda i: (i, 0)), (lambda i: (0, 0))
    return pl.pallas_call(
        _glu_kernel,
        grid=(m // TM,),
        in_specs=[pl.BlockSpec((TM, n), row), pl.BlockSpec((TM, n), row), pl.BlockSpec((1, n), fixed),
                  pl.BlockSpec((n, n), fixed), pl.BlockSpec((1, n), fixed)],
        out_specs=pl.BlockSpec((TM, n), row),
        out_shape=jax.ShapeDtypeStruct((m, n), BF16),
        compiler_params=_params("parallel"),
        name="s5_glu",
    )(y, u, d.reshape(1, n), w, b.reshape(1, n))


def _pool_prompt_kernel(cur_ref, halo_ref, w_ref, scale_ref, o_ref, buf, *, tile, tiles_per_seq):
    i = pl.program_id(0)
    first = (i % tiles_per_seq) == 0
    buf[0:POOL_HALO, :] = jnp.where(first, 0.0, halo_ref[...])
    buf[POOL_HALO:, :] = cur_ref[...]
    pos = (i % tiles_per_seq) * tile + lax.broadcasted_iota(jnp.int32, (tile, POOL_GROUP), 0)
    for gi, win in enumerate(POOL_WINDOWS):
        cols = slice(gi * POOL_GROUP, (gi + 1) * POOL_GROUP)
        x = buf[POOL_HALO:, cols]
        acc = jnp.zeros_like(x)
        for k in range(win):
            acc = acc + buf[POOL_HALO - k:POOL_HALO - k + tile, cols]
        cnt = jnp.minimum(pos + 1, win).astype(F32)
        m = (acc / cnt - x).astype(BF16)
        y = jnp.dot(m, w_ref[gi], preferred_element_type=F32)
        o_ref[:, cols] = (y * scale_ref[:, cols]).astype(o_ref.dtype)


def _pool_prompt(u, col_block, w, scale, bsz, seq):
    tile = POOL_TILE
    tiles_per_seq = seq // tile
    halo_blocks = tile // POOL_HALO
    n = POOL_W
    return pl.pallas_call(
        functools.partial(_pool_prompt_kernel, tile=tile, tiles_per_seq=tiles_per_seq),
        grid=(bsz * tiles_per_seq,),
        in_specs=[pl.BlockSpec((tile, n), lambda i: (i, col_block)),
                  pl.BlockSpec((POOL_HALO, n), lambda i: (jnp.maximum(i * halo_blocks - 1, 0), col_block)),
                  pl.BlockSpec((len(POOL_WINDOWS), POOL_GROUP, POOL_GROUP), lambda i: (0, 0, 0)),
                  pl.BlockSpec((1, n), lambda i: (0, 0))],
        out_specs=pl.BlockSpec((tile, n), lambda i: (i, 0)),
        out_shape=jax.ShapeDtypeStruct((bsz * seq, n), BF16),
        scratch_shapes=[pltpu.VMEM((POOL_HALO + tile, n), F32)],
        compiler_params=_params("parallel"),
        name="pool_prompt",
    )(u, u, w, scale.reshape(1, n))


def _pool_sample_kernel(xc_ref, w_ref, scale_ref, o_ref, *, steps, start_pos):
    for gi, win in enumerate(POOL_WINDOWS):
        cols = slice(gi * POOL_GROUP, (gi + 1) * POOL_GROUP)
        for t in range(steps):
            x = xc_ref[POOL_HALO + t, :, cols]
            acc = jnp.zeros_like(x)
            for k in range(win):
                acc = acc + xc_ref[POOL_HALO + t - k, :, cols]
            cnt = float(min(start_pos + t + 1, win))
            m = (acc / cnt - x).astype(BF16)
            y = jnp.dot(m, w_ref[gi], preferred_element_type=F32)
            o_ref[t, :, cols] = (y * scale_ref[:, cols]).astype(o_ref.dtype)


def _pool_sample(xc, w, scale, steps, start_pos):
    rows, bsz, n = xc.shape
    whole3 = lambda i: (0, 0, 0)
    return pl.pallas_call(
        functools.partial(_pool_sample_kernel, steps=steps, start_pos=start_pos),
        grid=(1,),
        in_specs=[pl.BlockSpec((rows, bsz, n), whole3),
                  pl.BlockSpec((len(POOL_WINDOWS), POOL_GROUP, POOL_GROUP), whole3),
                  pl.BlockSpec((1, n), lambda i: (0, 0))],
        out_specs=pl.BlockSpec((steps, bsz, n), whole3),
        out_shape=jax.ShapeDtypeStruct((steps, bsz, n), BF16),
        compiler_params=_params("arbitrary"),
        name="pool_sample",
    )(xc, w, scale.reshape(1, n))


def _merge_kernel(g0_ref, g1_ref, g2_ref, a_ref, b_ref, c_ref, wa_ref, wb_ref, wc_ref, o_ref):
    dot = lambda x, w: jnp.dot(x[...], w[...], preferred_element_type=F32)
    merged = (g0_ref[...].astype(F32) * dot(a_ref, wa_ref) + g1_ref[...].astype(F32) * dot(b_ref, wb_ref)
              + g2_ref[...].astype(F32) * dot(c_ref, wc_ref))
    o_ref[...] = merged.astype(o_ref.dtype)


def _merge(gates, o_ssm, o_mla, o_pool, w_ssm, w_mla, w_pool):
    m = gates.shape[0]
    tn = 512
    nb = D_MODEL // tn
    xspec = lambda k: pl.BlockSpec((TM, k), lambda i, j: (i, 0))
    wspec = lambda k: pl.BlockSpec((k, tn), lambda i, j: (0, j))
    gspec = lambda b: pl.BlockSpec((TM, tn), lambda i, j: (i, b * nb + j))
    return pl.pallas_call(
        _merge_kernel,
        grid=(m // TM, nb),
        in_specs=[gspec(0), gspec(1), gspec(2), xspec(SSM_W), xspec(N_HEADS * V_DIM), xspec(POOL_W),
                  wspec(SSM_W), wspec(N_HEADS * V_DIM), wspec(POOL_W)],
        out_specs=pl.BlockSpec((TM, tn), lambda i, j: (i, j)),
        out_shape=jax.ShapeDtypeStruct((m, D_MODEL), BF16),
        compiler_params=_params("parallel", "parallel"),
        name="merge",
    )(gates, gates, gates, o_ssm, o_mla, o_pool, w_ssm, w_mla, w_pool)


def _swiglu_kernel(x_ref, wg_ref, wu_ref, o_ref):
    x = x_ref[...]
    a = jnp.dot(x, wg_ref[...], preferred_element_type=F32)
    b = jnp.dot(x, wu_ref[...], preferred_element_type=F32)
    o_ref[...] = (a * _sigmoid(a) * b).astype(o_ref.dtype)


def _swiglu(h, wg, wu):
    m, k = h.shape
    n = wg.shape[1]
    tn = 512
    return pl.pallas_call(
        _swiglu_kernel,
        grid=(m // TM, n // tn),
        in_specs=[pl.BlockSpec((TM, k), lambda i, j: (i, 0)), pl.BlockSpec((k, tn), lambda i, j: (0, j)),
                  pl.BlockSpec((k, tn), lambda i, j: (0, j))],
        out_specs=pl.BlockSpec((TM, tn), lambda i, j: (i, j)),
        out_shape=jax.ShapeDtypeStruct((m, n), BF16),
        compiler_params=_params("parallel", "parallel"),
        name="swiglu",
    )(h, wg, wu)


def _rope_tables():
    inv = 1.0 / (ROPE_BASE ** (jnp.arange(0, QK_ROPE, 2, dtype=F32) / QK_ROPE))
    pos = jnp.concatenate([jnp.tile(jnp.arange(SEQ), BATCH), jnp.tile(PAST_LEN + jnp.arange(DEC_SEQ), DEC_BATCH)])
    ang = pos.astype(F32)[:, None] * inv[None, :]
    zeros = jnp.zeros((N_TOK, LANES - QK_ROPE), F32)
    cos_t = jnp.concatenate([jnp.cos(ang), jnp.cos(ang), zeros], axis=1)
    sin_t = jnp.concatenate([jnp.sin(ang), jnp.sin(ang), zeros], axis=1)
    return cos_t, sin_t


def _with_rotate_half(w):
    half = QK_ROPE // 2
    return jnp.concatenate([w, -w[:, half:], w[:, :half]], axis=1)


def _regroup(u, bsz, n_chunks, chunk):
    x = u.reshape(bsz, n_chunks, chunk, SSM_GROUPS, SSM_GROUP)
    return x.transpose(3, 0, 1, 2, 4).reshape(SSM_GROUPS, bsz * n_chunks, chunk * SSM_GROUP)


def _ungroup(y, bsz, n_chunks, chunk):
    x = y.reshape(SSM_GROUPS, bsz, n_chunks, chunk, SSM_GROUP)
    return x.transpose(1, 2, 3, 0, 4).reshape(bsz * n_chunks * chunk, SSM_W)


def _split_state(s):
    s = s.transpose(1, 0, 2)
    return s[..., :SSM_STATE], s[..., SSM_STATE:]


def kernel(x_prompt, x_sample, cache_kv_latent, cache_k_rope, state_ssm_re, state_ssm_im, state_pool_buf,
           page_table, norm_mix, norm_ffn, norm_final, w_in,
           ssm_a_re, ssm_a_im, ssm_log_dt, ssm_b_re, ssm_b_im, ssm_c_re, ssm_c_im, ssm_d, ssm_w_glu, ssm_b_glu,
           mla_q_norm, mla_w_q_b, mla_kv_norm, mla_w_kv_b, pool_w, pool_scale,
           w_br_ssm, w_br_mla, w_br_pool, w_out, ffn_w_gate, ffn_w_up, ffn_w_down):
    cos_t, sin_t = _rope_tables()
    pt_flat = page_table.reshape(-1).astype(jnp.int32)
    x = jnp.concatenate([x_prompt.reshape(N_PROMPT, D_MODEL), x_sample.reshape(N_SAMPLE, D_MODEL)], axis=0)
    n_chunks = SEQ // S5_CHUNK
    zero_state = jnp.zeros((SSM_GROUPS, BATCH, 2 * SSM_STATE), F32)
    outs = {k: [] for k in ("p_lat", "p_kr", "p_re", "p_im", "p_buf", "s_lat", "s_kr", "s_re", "s_im", "s_buf")}

    for l in range(DEPTH):
        wi = w_in[l]
        w_main = jnp.concatenate([wi[:, OFF_SSM:OFF_Q], wi[:, OFF_POOL:OFF_GATE]], axis=1).astype(BF16)
        w_lat = wi[:, OFF_Q:OFF_KR].astype(BF16)
        w_kr = _with_rotate_half(wi[:, OFF_KR:OFF_POOL]).astype(BF16)
        w_gates = wi[:, OFF_GATE:].astype(BF16)
        wq = mla_w_q_b[l].reshape(Q_LORA, N_HEADS, QK_NOPE + QK_ROPE)
        wq = jnp.concatenate([wq[..., :QK_NOPE], wq[..., QK_NOPE:], -wq[..., QK_NOPE + QK_ROPE // 2:],
                              wq[..., QK_NOPE:QK_NOPE + QK_ROPE // 2]], axis=-1)
        wq = wq.reshape(Q_LORA, N_HEADS * HEAD_PAD).astype(BF16)
        wkv = mla_w_kv_b[l].reshape(KV_LORA, N_HEADS, QK_NOPE + V_DIM)
        w_uk, w_uv = wkv[..., :QK_NOPE], wkv[..., QK_NOPE:]
        wk = w_uk.reshape(KV_LORA, N_HEADS * QK_NOPE).astype(BF16)
        wv = w_uv.reshape(KV_LORA, N_HEADS * V_DIM).astype(BF16)
        w_uk_t = w_uk.transpose(1, 2, 0).astype(BF16)
        w_uv_h = w_uv.transpose(1, 0, 2).astype(BF16)

        h = _rmsnorm(x, norm_mix[l], BF16)
        u_main = _linear(h, w_main, F32, tn=512, name="in_proj_main")
        u_ssm, u_pool = u_main[:, :SSM_W], u_main[:, SSM_W:]
        gates = _linear(h, w_gates, BF16, tn=512, act="sigmoid", name="in_proj_gates")
        lat32, lat16 = _latent_proj(h, w_lat, jnp.stack([mla_q_norm[l], mla_kv_norm[l]]).reshape(2, 1, Q_LORA))
        c_q16, c_kv32, c_kv16 = lat16[:, :Q_LORA], lat32[:, Q_LORA:], lat16[:, Q_LORA:]
        kr32, kr16 = _krope_proj(h, w_kr, cos_t, sin_t)
        q_full = _q_proj(c_q16, wq, cos_t, sin_t)
        outs["p_lat"].append(c_kv32[:N_PROMPT].reshape(BATCH, SEQ, KV_LORA))
        outs["s_lat"].append(c_kv32[N_PROMPT:].reshape(DEC_BATCH, DEC_SEQ, KV_LORA))
        outs["p_kr"].append(kr32[:N_PROMPT, :QK_ROPE].reshape(BATCH, SEQ, QK_ROPE))
        outs["s_kr"].append(kr32[N_PROMPT:, :QK_ROPE].reshape(DEC_BATCH, DEC_SEQ, QK_ROPE))

        ssm = (ssm_a_re[l], ssm_a_im[l], ssm_log_dt[l], ssm_b_re[l], ssm_b_im[l], ssm_c_re[l], ssm_c_im[l])
        u16 = u_ssm.astype(BF16)
        y_p, fin_p = _s5_core(_regroup(u16[:N_PROMPT], BATCH, n_chunks, S5_CHUNK), _s5_prep(*ssm, S5_CHUNK),
                              zero_state, BATCH, n_chunks)
        s0 = jnp.concatenate([state_ssm_re[l], state_ssm_im[l]], axis=-1).astype(F32).transpose(1, 0, 2)
        y_s, fin_s = _s5_core(_regroup(u16[N_PROMPT:], DEC_BATCH, 1, DEC_SEQ), _s5_prep(*ssm, DEC_SEQ),
                              s0, DEC_BATCH, 1)
        y_ssm = jnp.concatenate([_ungroup(y_p, BATCH, n_chunks, S5_CHUNK), _ungroup(y_s, DEC_BATCH, 1, DEC_SEQ)])
        o_ssm = _glu(y_ssm, u_main, ssm_d[l], ssm_w_glu[l].astype(BF16), ssm_b_glu[l])
        for key, fin in (("p", fin_p), ("s", fin_s)):
            re, im = _split_state(fin)
            outs[key + "_re"].append(re)
            outs[key + "_im"].append(im)

        k_full, v = _kv_proj(lat16, wk, wv, kr16, N_PROMPT)
        o_mla_p = _flash_attention(q_full, k_full, v, BATCH, SEQ)
        q_s = q_full[N_PROMPT:]
        q_lat = _headwise_linear(q_s, w_uk_t, lambda hd: 2 * hd, "q_absorb")
        q_lat = q_lat.reshape(DEC_BATCH, DEC_SEQ * N_HEADS, KV_LORA)
        q_rope = q_s.reshape(N_SAMPLE, N_HEADS, HEAD_PAD)[:, :, QK_NOPE:QK_NOPE + QK_ROPE]
        q_rope = q_rope.reshape(DEC_BATCH, DEC_SEQ * N_HEADS, QK_ROPE)
        pad_rows = ((0, 0), (0, 8 - DEC_SEQ), (0, 0))
        c_self = jnp.pad(c_kv16[N_PROMPT:].reshape(DEC_BATCH, DEC_SEQ, KV_LORA), pad_rows)
        r_self = jnp.pad(kr16[N_PROMPT:, :QK_ROPE].reshape(DEC_BATCH, DEC_SEQ, QK_ROPE), pad_rows)
        o_lat = _paged_attention(l, pt_flat, q_lat, q_rope, c_self, r_self, cache_kv_latent, cache_k_rope)
        o_mla_s = _headwise_linear(o_lat.reshape(N_SAMPLE, N_HEADS * KV_LORA), w_uv_h, lambda hd: hd, "v_absorb")
        o_mla = jnp.concatenate([o_mla_p, o_mla_s], axis=0)

        pw16 = pool_w[l].astype(BF16)
        o_pool_p = _pool_prompt(u_main, 1, pw16, pool_scale[l], BATCH, SEQ)
        u_pool_s = u_pool[N_PROMPT:].reshape(DEC_BATCH, DEC_SEQ, POOL_W)
        xc_s = jnp.concatenate([jnp.zeros((DEC_BATCH, POOL_HALO - POOL_BUF, POOL_W), F32),
                                state_pool_buf[l].astype(F32), u_pool_s], axis=1)
        o_pool_s = _pool_sample(xc_s.transpose(1, 0, 2), pw16, pool_scale[l], DEC_SEQ, PAST_LEN)
        o_pool = jnp.concatenate([o_pool_p, o_pool_s.transpose(1, 0, 2).reshape(N_SAMPLE, POOL_W)], axis=0)
        outs["p_buf"].append(u_pool[:N_PROMPT].reshape(BATCH, SEQ, POOL_W)[:, SEQ - POOL_BUF:])
        outs["s_buf"].append(xc_s[:, -POOL_BUF:])

        merged = _merge(gates, o_ssm, o_mla, o_pool, w_br_ssm[l].astype(BF16), w_br_mla[l].astype(BF16),
                        w_br_pool[l].astype(BF16))
        x = _linear(merged, w_out[l].astype(BF16), F32, tn=512, residual=x, name="out_proj")
        h2 = _rmsnorm(x, norm_ffn[l], BF16)
        act = _swiglu(h2, ffn_w_gate[l].astype(BF16), ffn_w_up[l].astype(BF16))
        x = _linear(act, ffn_w_down[l].astype(BF16), F32, tn=512, residual=x, name="ffn_down")

    y = _rmsnorm(x, norm_final, F32)
    stack = lambda key: jnp.stack(outs[key])
    return (y[:N_PROMPT].reshape(BATCH, SEQ, D_MODEL), y[N_PROMPT:].reshape(DEC_BATCH, DEC_SEQ, D_MODEL),
            stack("p_lat"), stack("p_kr"), stack("p_re"), stack("p_im"), stack("p_buf"),
            stack("s_lat"), stack("s_kr"), stack("s_re"), stack("s_im"), stack("s_buf"))
```

```python
import functools
import math

import jax
import jax.numpy as jnp
from jax import lax
from jax.experimental import pallas as pl
from jax.experimental.pallas import tpu as pltpu

F32 = jnp.float32
BF16 = jnp.bfloat16

D_MODEL = 2048
BATCH = 4
SEQ = 2048
DEPTH = 4
DEC_BATCH = 128
DEC_SEQ = 4
PAST_LEN = 8192
PAGE_SIZE = 128
N_PAGES = PAST_LEN // PAGE_SIZE
SSM_W = 1024
SSM_GROUP = 16
SSM_GROUPS = SSM_W // SSM_GROUP
SSM_STATE = 64
N_HEADS = 16
Q_LORA = 512
KV_LORA = 512
QK_NOPE = 128
QK_ROPE = 64
V_DIM = 128
ROPE_BASE = 10000.0
ATTN_SCALE = (QK_NOPE + QK_ROPE) ** -0.5
NEG = -1e30
POOL_W = 1024
POOL_WINDOWS = (2, 4, 8, 16)
POOL_GROUP = POOL_W // len(POOL_WINDOWS)
POOL_BUF = max(POOL_WINDOWS) - 1
POOL_HALO = 16
N_BRANCH = 3
D_FF = (((8 * D_MODEL + 2) // 3 + 255) // 256) * 256
EPS = 1e-6
OFF_SSM = 0
OFF_Q = OFF_SSM + SSM_W
OFF_KV = OFF_Q + Q_LORA
OFF_KR = OFF_KV + KV_LORA
OFF_POOL = OFF_KR + QK_ROPE
OFF_GATE = OFF_POOL + POOL_W

N_PROMPT = BATCH * SEQ
N_SAMPLE = DEC_BATCH * DEC_SEQ
N_TOK = N_PROMPT + N_SAMPLE
HEAD_PAD = 256
LANES = 128
S5_CHUNK = 16
S5_GROUPS_PER_STEP = 8
PAGES_PER_STEP = 16
POOL_TILE = 256
TM = 512
VMEM_LIMIT = 56 * 1024 * 1024


def _params(*sem):
    return pltpu.CompilerParams(dimension_semantics=sem, vmem_limit_bytes=VMEM_LIMIT)


def _sigmoid(x):
    return 0.5 * jnp.tanh(0.5 * x) + 0.5


def _rmsnorm_kernel(x_ref, g_ref, o_ref):
    x = x_ref[...]
    ms = jnp.mean(x * x, axis=-1, keepdims=True)
    o_ref[...] = (x * lax.rsqrt(ms + EPS) * g_ref[...]).astype(o_ref.dtype)


def _rmsnorm(x, g, out_dtype):
    m, d = x.shape
    return pl.pallas_call(
        _rmsnorm_kernel,
        grid=(m // TM,),
        in_specs=[pl.BlockSpec((TM, d), lambda i: (i, 0)), pl.BlockSpec((1, d), lambda i: (0, 0))],
        out_specs=pl.BlockSpec((TM, d), lambda i: (i, 0)),
        out_shape=jax.ShapeDtypeStruct((m, d), out_dtype),
        compiler_params=_params("parallel"),
        name="rmsnorm",
    )(x, g.reshape(1, d))


def _linear_kernel(*refs, act, has_res):
    x_ref, w_ref = refs[0], refs[1]
    o_ref = refs[-1]
    z = jnp.dot(x_ref[...], w_ref[...], preferred_element_type=F32)
    if act == "sigmoid":
        z = _sigmoid(z)
    if has_res:
        z = refs[2][...] + z
    o_ref[...] = z.astype(o_ref.dtype)


def _linear(x, w, out_dtype, *, tn, act=None, residual=None, name="linear"):
    m, k = x.shape
    n = w.shape[1]
    tm = min(TM, m)
    in_specs = [pl.BlockSpec((tm, k), lambda i, j: (i, 0)), pl.BlockSpec((k, tn), lambda i, j: (0, j))]
    args = [x, w]
    if residual is not None:
        in_specs.append(pl.BlockSpec((tm, tn), lambda i, j: (i, j)))
        args.append(residual)
    return pl.pallas_call(
        functools.partial(_linear_kernel, act=act, has_res=residual is not None),
        grid=(m // tm, n // tn),
        in_specs=in_specs,
        out_specs=pl.BlockSpec((tm, tn), lambda i, j: (i, j)),
        out_shape=jax.ShapeDtypeStruct((m, n), out_dtype),
        compiler_params=_params("parallel", "parallel"),
        name=name,
    )(*args)


def _latent_kernel(x_ref, w_ref, g_ref, o32_ref, o16_ref):
    z = jnp.dot(x_ref[...], w_ref[...], preferred_element_type=F32)
    ms = jnp.mean(z * z, axis=-1, keepdims=True)
    y = z * lax.rsqrt(ms + EPS) * g_ref[0]
    o32_ref[...] = y
    o16_ref[...] = y.astype(BF16)


def _latent_proj(h, w, gains):
    m, k = h.shape
    n = w.shape[1]
    tn = Q_LORA
    return pl.pallas_call(
        _latent_kernel,
        grid=(m // TM, n // tn),
        in_specs=[pl.BlockSpec((TM, k), lambda i, j: (i, 0)),
                  pl.BlockSpec((k, tn), lambda i, j: (0, j)),
                  pl.BlockSpec((1, 1, tn), lambda i, j: (j, 0, 0))],
        out_specs=[pl.BlockSpec((TM, tn), lambda i, j: (i, j)), pl.BlockSpec((TM, tn), lambda i, j: (i, j))],
        out_shape=[jax.ShapeDtypeStruct((m, n), F32), jax.ShapeDtypeStruct((m, n), BF16)],
        compiler_params=_params("parallel", "parallel"),
        name="latent_proj",
    )(h, w, gains)


def _rope_tile(z, cos_t, sin_t):
    return z * cos_t + pltpu.roll(z, 64, axis=1) * sin_t


def _krope_kernel(x_ref, w_ref, cos_ref, sin_ref, o32_ref, o16_ref):
    z = jnp.dot(x_ref[...], w_ref[...], preferred_element_type=F32)
    y = _rope_tile(z, cos_ref[...], sin_ref[...])
    o32_ref[...] = y
    o16_ref[...] = y.astype(BF16)


def _krope_proj(h, w, cos_t, sin_t):
    m, k = h.shape
    row = lambda i: (i, 0)
    return pl.pallas_call(
        _krope_kernel,
        grid=(m // TM,),
        in_specs=[pl.BlockSpec((TM, k), row), pl.BlockSpec((k, LANES), lambda i: (0, 0)),
                  pl.BlockSpec((TM, LANES), row), pl.BlockSpec((TM, LANES), row)],
        out_specs=[pl.BlockSpec((TM, LANES), row), pl.BlockSpec((TM, LANES), row)],
        out_shape=[jax.ShapeDtypeStruct((m, LANES), F32), jax.ShapeDtypeStruct((m, LANES), BF16)],
        compiler_params=_params("parallel"),
        name="krope_proj",
    )(h, w, cos_t, sin_t)


def _q_kernel(x_ref, w_ref, cos_ref, sin_ref, o_ref, *, heads):
    z = jnp.dot(x_ref[...], w_ref[...], preferred_element_type=F32) * ATTN_SCALE
    cos_t, sin_t = cos_ref[...], sin_ref[...]
    for h in range(heads):
        base = h * HEAD_PAD
        o_ref[:, base:base + QK_NOPE] = z[:, base:base + QK_NOPE].astype(BF16)
        o_ref[:, base + QK_NOPE:base + HEAD_PAD] = _rope_tile(
            z[:, base + QK_NOPE:base + HEAD_PAD], cos_t, sin_t).astype(BF16)


def _q_proj(c_q, w, cos_t, sin_t):
    m, k = c_q.shape
    n = w.shape[1]
    heads = 2
    tn = heads * HEAD_PAD
    return pl.pallas_call(
        functools.partial(_q_kernel, heads=heads),
        grid=(m // TM, n // tn),
        in_specs=[pl.BlockSpec((TM, k), lambda i, j: (i, 0)), pl.BlockSpec((k, tn), lambda i, j: (0, j)),
                  pl.BlockSpec((TM, LANES), lambda i, j: (i, 0)), pl.BlockSpec((TM, LANES), lambda i, j: (i, 0))],
        out_specs=pl.BlockSpec((TM, tn), lambda i, j: (i, j)),
        out_shape=jax.ShapeDtypeStruct((m, n), BF16),
        compiler_params=_params("parallel", "parallel"),
        name="q_proj",
    )(c_q, w, cos_t, sin_t)


def _kv_kernel(x_ref, wk_ref, wv_ref, kr_ref, k_ref, v_ref, *, heads):
    x = x_ref[...]
    zk = jnp.dot(x, wk_ref[...], preferred_element_type=F32).astype(BF16)
    v_ref[...] = jnp.dot(x, wv_ref[...], preferred_element_type=F32).astype(BF16)
    kr = kr_ref[...]
    for h in range(heads):
        k_ref[:, h * HEAD_PAD:h * HEAD_PAD + QK_NOPE] = zk[:, h * QK_NOPE:(h + 1) * QK_NOPE]
        k_ref[:, h * HEAD_PAD + QK_NOPE:(h + 1) * HEAD_PAD] = kr


def _kv_proj(lat16, wk, wv, kr16, m):
    k = KV_LORA
    heads = 2
    return pl.pallas_call(
        functools.partial(_kv_kernel, heads=heads),
        grid=(m // TM, N_HEADS // heads),
        in_specs=[pl.BlockSpec((TM, k), lambda i, j: (i, 1)),
                  pl.BlockSpec((k, heads * QK_NOPE), lambda i, j: (0, j)),
                  pl.BlockSpec((k, heads * V_DIM), lambda i, j: (0, j)),
                  pl.BlockSpec((TM, LANES), lambda i, j: (i, 0))],
        out_specs=[pl.BlockSpec((TM, heads * HEAD_PAD), lambda i, j: (i, j)),
                   pl.BlockSpec((TM, heads * V_DIM), lambda i, j: (i, j))],
        out_shape=[jax.ShapeDtypeStruct((m, N_HEADS * HEAD_PAD), BF16),
                   jax.ShapeDtypeStruct((m, N_HEADS * V_DIM), BF16)],
        compiler_params=_params("parallel", "parallel"),
        name="kv_proj",
    )(lat16, wk, wv, kr16)


FLASH_T = 512


FLASH_HEADS = 2


def _flash_kernel(q_ref, k_ref, v_ref, o_ref, m_sc, l_sc, acc_sc, *, t):
    qi = pl.program_id(2)
    m_sc[...] = jnp.full_like(m_sc, NEG)
    l_sc[...] = jnp.zeros_like(l_sc)
    acc_sc[...] = jnp.zeros_like(acc_sc)

    def block(j, masked):
        start = pl.multiple_of(j * t, t)
        for h in range(FLASH_HEADS):
            q = q_ref[:, h * HEAD_PAD:(h + 1) * HEAD_PAD]
            k = k_ref[pl.ds(start, t), h * HEAD_PAD:(h + 1) * HEAD_PAD]
            v = v_ref[pl.ds(start, t), h * V_DIM:(h + 1) * V_DIM]
            s = lax.dot_general(q, k, (((1,), (1,)), ((), ())), preferred_element_type=F32)
            if masked:
                row = lax.broadcasted_iota(jnp.int32, s.shape, 0)
                col = lax.broadcasted_iota(jnp.int32, s.shape, 1)
                s = jnp.where(col <= row, s, NEG)
            m_prev = m_sc[h]
            m_new = jnp.maximum(m_prev, jnp.max(s, axis=-1, keepdims=True))
            corr = jnp.exp(m_prev - m_new)
            p = jnp.exp(s - m_new)
            l_sc[h] = corr * l_sc[h] + jnp.sum(p, axis=-1, keepdims=True)
            acc_sc[h] = corr * acc_sc[h] + jnp.dot(p.astype(BF16), v, preferred_element_type=F32)
            m_sc[h] = m_new

    pl.loop(0, qi)(lambda j: block(j, False))
    block(qi, True)
    for h in range(FLASH_HEADS):
        o_ref[:, h * V_DIM:(h + 1) * V_DIM] = (acc_sc[h] / l_sc[h]).astype(o_ref.dtype)


def _flash_attention(q_full, k_full, v, bsz, seq):
    t = FLASH_T
    nblk = seq // t
    hp = FLASH_HEADS
    return pl.pallas_call(
        functools.partial(_flash_kernel, t=t),
        grid=(bsz, N_HEADS // hp, nblk),
        in_specs=[pl.BlockSpec((t, hp * HEAD_PAD), lambda b, h, i: (b * nblk + i, h)),
                  pl.BlockSpec((seq, hp * HEAD_PAD), lambda b, h, i: (b, h)),
                  pl.BlockSpec((seq, hp * V_DIM), lambda b, h, i: (b, h))],
        out_specs=pl.BlockSpec((t, hp * V_DIM), lambda b, h, i: (b * nblk + i, h)),
        out_shape=jax.ShapeDtypeStruct((bsz * seq, N_HEADS * V_DIM), BF16),
        scratch_shapes=[pltpu.VMEM((hp, t, 1), F32), pltpu.VMEM((hp, t, 1), F32), pltpu.VMEM((hp, t, V_DIM), F32)],
        compiler_params=_params("parallel", "parallel", "arbitrary"),
        name="flash_attention",
    )(q_full, k_full, v)


def _headwise_kernel(x_ref, w_ref, o_ref):
    o_ref[...] = jnp.dot(x_ref[...], w_ref[0], preferred_element_type=F32).astype(o_ref.dtype)


def _headwise_linear(x, w, x_block_of_head, name):
    m = x.shape[0]
    heads, kh, nh = w.shape
    return pl.pallas_call(
        _headwise_kernel,
        grid=(heads,),
        in_specs=[pl.BlockSpec((m, kh), lambda h: (0, x_block_of_head(h))),
                  pl.BlockSpec((1, kh, nh), lambda h: (h, 0, 0))],
        out_specs=pl.BlockSpec((m, nh), lambda h: (0, h)),
        out_shape=jax.ShapeDtypeStruct((m, heads * nh), BF16),
        compiler_params=_params("parallel"),
        name=name,
    )(x, w)


def _paged_kernel(pt_ref, qlat_ref, qrope_ref, cself_ref, rself_ref, lat_hbm, rope_hbm, o_ref,
                  m_sc, l_sc, acc_sc, lat_buf, rope_buf, lat16, rope16, sems, *, layer, pages, chunks_per_seq):
    c = pl.program_id(0)
    j = c % chunks_per_seq
    slot = c % 2
    nt = (((1,), (1,)), ((), ()))

    def page_copies(chunk, buf_slot):
        copies = []
        for p in range(pages):
            page = pt_ref[chunk * pages + p]
            copies.append(pltpu.make_async_copy(
                lat_hbm.at[layer, page], lat_buf.at[buf_slot, pl.ds(p * PAGE_SIZE, PAGE_SIZE)],
                sems.at[0, buf_slot]))
            copies.append(pltpu.make_async_copy(rope_hbm.at[layer, page], rope_buf.at[buf_slot, p],
                                                sems.at[1, buf_slot]))
        return copies

    @pl.when(c == 0)
    def _():
        for cp in page_copies(0, 0):
            cp.start()

    @pl.when(c + 1 < pl.num_programs(0))
    def _():
        for cp in page_copies(c + 1, 1 - slot):
            cp.start()

    @pl.when(j == 0)
    def _():
        m_sc[...] = jnp.full_like(m_sc, NEG)
        l_sc[...] = jnp.zeros_like(l_sc)
        acc_sc[...] = jnp.zeros_like(acc_sc)

    for cp in page_copies(c, slot):
        cp.wait()
    for p in range(pages):
        rows = pl.ds(p * PAGE_SIZE, PAGE_SIZE)
        lat16[rows, :] = lat_buf[slot, rows, :].astype(BF16)
        rope16[:, p * PAGE_SIZE:(p + 1) * PAGE_SIZE] = rope_buf[slot, p].astype(BF16)

    qlat, qrope = qlat_ref[0], qrope_ref[0]

    def accumulate(s, values):
        m_prev = m_sc[...]
        m_new = jnp.maximum(m_prev, jnp.max(s, axis=-1, keepdims=True))
        corr = jnp.exp(m_prev - m_new)
        p = jnp.exp(s - m_new)
        l_sc[...] = corr * l_sc[...] + jnp.sum(p, axis=-1, keepdims=True)
        acc_sc[...] = corr * acc_sc[...] + jnp.dot(p.astype(BF16), values, preferred_element_type=F32)
        m_sc[...] = m_new

    lat = lat16[...]
    s = (lax.dot_general(qlat, lat, nt, preferred_element_type=F32)
         + jnp.dot(qrope, rope16[...], preferred_element_type=F32))
    accumulate(s, lat)

    @pl.when(j == chunks_per_seq - 1)
    def _():
        cself = cself_ref[0]
        s_self = (lax.dot_general(qlat, cself, nt, preferred_element_type=F32)
                  + lax.dot_general(qrope, rself_ref[0], nt, preferred_element_type=F32))
        qpos = lax.broadcasted_iota(jnp.int32, s_self.shape, 0) // N_HEADS
        kpos = lax.broadcasted_iota(jnp.int32, s_self.shape, 1)
        accumulate(jnp.where(kpos <= qpos, s_self, NEG), cself)
        o_ref[0] = (acc_sc[...] / l_sc[...]).astype(o_ref.dtype)


def _paged_attention(layer, page_table_flat, q_lat, q_rope, c_self, r_self, cache_lat, cache_rope_t):
    pages = PAGES_PER_STEP
    chunks_per_seq = N_PAGES // pages
    rows = DEC_SEQ * N_HEADS
    self_rows = c_self.shape[1]
    per_b = lambda c, pt: (c // chunks_per_seq, 0, 0)
    in_specs = [pl.BlockSpec((1, rows, KV_LORA), per_b), pl.BlockSpec((1, rows, QK_ROPE), per_b),
                pl.BlockSpec((1, self_rows, KV_LORA), per_b), pl.BlockSpec((1, self_rows, QK_ROPE), per_b),
                pl.BlockSpec(memory_space=pl.ANY), pl.BlockSpec(memory_space=pl.ANY)]
    return pl.pallas_call(
        functools.partial(_paged_kernel, layer=layer, pages=pages, chunks_per_seq=chunks_per_seq),
        grid_spec=pltpu.PrefetchScalarGridSpec(
            num_scalar_prefetch=1,
            grid=(DEC_BATCH * chunks_per_seq,),
            in_specs=in_specs,
            out_specs=pl.BlockSpec((1, rows, KV_LORA), per_b),
            scratch_shapes=[pltpu.VMEM((rows, 1), F32), pltpu.VMEM((rows, 1), F32),
                            pltpu.VMEM((rows, KV_LORA), F32),
                            pltpu.VMEM((2, pages * PAGE_SIZE, KV_LORA), F32),
                            pltpu.VMEM((2, pages, QK_ROPE, PAGE_SIZE), F32),
                            pltpu.VMEM((pages * PAGE_SIZE, KV_LORA), BF16),
                            pltpu.VMEM((QK_ROPE, pages * PAGE_SIZE), BF16),
                            pltpu.SemaphoreType.DMA((2, 2))]),
        out_shape=jax.ShapeDtypeStruct((DEC_BATCH, rows, KV_LORA), BF16),
        compiler_params=_params("arbitrary"),
        name="paged_attention",
    )(page_table_flat, q_lat, q_rope, c_self, r_self, cache_lat, cache_rope_t)


def _cexp(k, dt, a_re, a_im):
    mag = jnp.exp(k * dt * a_re)
    ang = k * dt * a_im
    return mag * jnp.cos(ang), mag * jnp.sin(ang)


def _cmul(x_re, x_im, y_re, y_im):
    return x_re * y_re - x_im * y_im, x_re * y_im + x_im * y_re


S5_TABLE_ROWS = 16


def _s5_prep_kernel(a_re, a_im, dt_ref, bt_re, bt_im, ct_re, ct_im,
                    toep_ref, bst_re_ref, bst_im_ref, cst_re_ref, cst_im_ref, pw_re_ref, pw_im_ref, *, chunk):
    groups = a_re.shape[0]
    ti = chunk * SSM_GROUP
    hi = lax.Precision.HIGHEST
    nt = (((1,), (1,)), ((), ()))
    expand = (lax.broadcasted_iota(jnp.int32, (ti, S5_TABLE_ROWS), 0) // SSM_GROUP
              == lax.broadcasted_iota(jnp.int32, (ti, S5_TABLE_ROWS), 1)).astype(F32)
    rep = lambda x: jnp.dot(expand, x, precision=hi, preferred_element_type=F32)
    step = lax.broadcasted_iota(jnp.int32, (S5_TABLE_ROWS, SSM_STATE), 0).astype(F32)
    s_idx = lax.broadcasted_iota(jnp.int32, (ti, ti), 0) // SSM_GROUP
    t_idx = lax.broadcasted_iota(jnp.int32, (ti, ti), 1) // SSM_GROUP
    k_pow = chunk * jnp.left_shift(1, lax.broadcasted_iota(jnp.int32, (8, SSM_STATE), 0)).astype(F32)

    @pl.loop(0, groups)
    def _(g):
        ar, ai, dt = a_re[g], a_im[g], dt_ref[g]
        ab_re, ab_im = _cexp(1.0, dt, ar, ai)
        den = ar * ar + ai * ai
        n_re, n_im = ab_re - 1.0, ab_im
        c_re = (n_re * ar + n_im * ai) / den
        c_im = (n_im * ar - n_re * ai) / den
        bb_re, bb_im = _cmul(c_re, c_im, bt_re[g], bt_im[g])
        em_re, em_im = _cexp(-step, dt, ar, ai)
        ep_re, ep_im = _cexp(step, dt, ar, ai)
        p_re, p_im = _cmul(rep(em_re), rep(em_im), bb_re, bb_im)
        h_re, h_im = _cmul(*_cexp(chunk - 1.0, dt, ar, ai), p_re, p_im)
        bst_re_ref[g] = h_re
        bst_im_ref[g] = h_im
        q_re, q_im = _cmul(rep(ep_re), rep(ep_im), ct_re[g], ct_im[g])
        r_re, r_im = _cmul(ab_re, ab_im, q_re, q_im)
        cst_re_ref[g] = r_re
        cst_im_ref[g] = -r_im
        toep = (lax.dot_general(p_re, q_re, nt, preferred_element_type=F32, precision=hi)
                - lax.dot_general(p_im, q_im, nt, preferred_element_type=F32, precision=hi))
        toep_ref[g] = jnp.where(t_idx >= s_idx, toep, 0.0).astype(BF16)
        pw_re, pw_im = _cexp(k_pow, dt, ar, ai)
        pw_re_ref[g] = pw_re
        pw_im_ref[g] = pw_im


def _s5_prep(a_re, a_im, log_dt, b_re, b_im, c_re, c_im, chunk):
    g, p, ti = SSM_GROUPS, SSM_STATE, chunk * SSM_GROUP
    assert chunk <= S5_TABLE_ROWS
    gs = S5_GROUPS_PER_STEP
    dt = jnp.exp(log_dt.astype(F32))
    vec = lambda x: x.astype(F32).reshape(g, 1, p)
    bt = lambda x: jnp.tile(jnp.swapaxes(x.astype(F32), 1, 2), (1, chunk, 1))
    ct = lambda x: jnp.tile(x.astype(F32), (1, chunk, 1))
    args = [vec(a_re), vec(a_im), jnp.broadcast_to(dt.reshape(g, 1, 1), (g, 1, p)), bt(b_re), bt(b_im),
            ct(c_re), ct(c_im)]
    blk = lambda shape: pl.BlockSpec((gs,) + shape, lambda i: (i, 0, 0))
    in_shapes = [(1, p), (1, p), (1, p), (ti, p), (ti, p), (ti, p), (ti, p)]
    out_shapes = [(ti, ti), (ti, p), (ti, p), (ti, p), (ti, p), (8, p), (8, p)]
    out_dtypes = [BF16, F32, F32, F32, F32, F32, F32]
    toep, bst_re, bst_im, cst_re, cst_im, pw_re, pw_im = pl.pallas_call(
        functools.partial(_s5_prep_kernel, chunk=chunk),
        grid=(g // gs,),
        in_specs=[blk(s) for s in in_shapes],
        out_specs=[blk(s) for s in out_shapes],
        out_shape=[jax.ShapeDtypeStruct((g,) + s, d) for s, d in zip(out_shapes, out_dtypes)],
        compiler_params=_params("parallel"),
        name="s5_prep",
    )(*args)
    bst = jnp.concatenate([bst_re, bst_im], axis=-1).astype(BF16)
    cst = jnp.swapaxes(jnp.concatenate([cst_re, cst_im], axis=-1), 1, 2).astype(BF16)
    pwa = jnp.concatenate([pw_re, pw_re], axis=-1)
    pwb = jnp.concatenate([-pw_im, pw_im], axis=-1)
    return toep, bst, cst, pwa, pwb


def _s5_core_kernel(a_ref, toep_ref, bst_ref, cst_ref, pwa_ref, pwb_ref, s0_ref, y_ref, sfin_ref, *scratch,
                    batch, n_chunks):
    groups = a_ref.shape[0]
    rows = batch * n_chunks
    pad = 64

    @pl.loop(0, groups)
    def _(g):
        a = a_ref[g]
        y = jnp.dot(a, toep_ref[g], preferred_element_type=F32)
        s_loc = jnp.dot(a, bst_ref[g], preferred_element_type=F32)
        if n_chunks == 1:
            s0 = s0_ref[g]
            s_prev = s0
            s_end = s0 * pwa_ref[g, 0:1, :] + pltpu.roll(s0, 64, axis=1) * pwb_ref[g, 0:1, :] + s_loc
            sfin_ref[g] = s_end
        else:
            buf = scratch[0]
            c_idx = lax.broadcasted_iota(jnp.int32, (rows, 2 * SSM_STATE), 0) % n_chunks
            buf[0:pad, :] = jnp.zeros((pad, 2 * SSM_STATE), F32)
            x = s_loc
            shift, k = 1, 0
            while shift < n_chunks:
                buf[pad:pad + rows, :] = x
                xs = buf[pad - shift:pad - shift + rows, :]
                inc = xs * pwa_ref[g, k:k + 1, :] + pltpu.roll(xs, 64, axis=1) * pwb_ref[g, k:k + 1, :]
                x = x + jnp.where(c_idx >= shift, inc, 0.0)
                shift, k = shift * 2, k + 1
            buf[pad:pad + rows, :] = x
            s_prev = jnp.where(c_idx >= 1, buf[pad - 1:pad - 1 + rows, :], 0.0)
            for b in range(batch):
                sfin_ref[g, b:b + 1, :] = x[(b + 1) * n_chunks - 1:(b + 1) * n_chunks, :]
        y_ref[g] = y + jnp.dot(s_prev.astype(BF16), cst_ref[g], preferred_element_type=F32)


def _s5_core(a, ops, s0, batch, n_chunks):
    toep, bst, cst, pwa, pwb = ops
    g, rows, ti = a.shape
    gs = S5_GROUPS_PER_STEP
    p2 = 2 * SSM_STATE
    assert n_chunks == 1 or n_chunks <= 128
    blk = lambda shape: pl.BlockSpec((gs,) + shape, lambda i: (i, 0, 0))
    scratch = [] if n_chunks == 1 else [pltpu.VMEM((64 + rows, p2), F32)]
    return pl.pallas_call(
        functools.partial(_s5_core_kernel, batch=batch, n_chunks=n_chunks),
        grid=(g // gs,),
        in_specs=[blk((rows, ti)), blk((ti, ti)), blk((ti, p2)), blk((p2, ti)), blk((8, p2)), blk((8, p2)),
                  blk((batch, p2))],
        out_specs=[blk((rows, ti)), blk((batch, p2))],
        out_shape=[jax.ShapeDtypeStruct((g, rows, ti), F32), jax.ShapeDtypeStruct((g, batch, p2), F32)],
        scratch_shapes=scratch,
        compiler_params=_params("parallel"),
        name="s5_core",
    )(a, toep, bst, cst, pwa, pwb, s0)


def _glu_kernel(y_ref, u_ref, d_ref, w_ref, b_ref, o_ref):
    x = y_ref[...] + d_ref[...] * u_ref[...]
    g = 0.5 * x * (1.0 + jnp.tanh(math.sqrt(2.0 / math.pi) * (x + 0.044715 * (x * x * x))))
    z = jnp.dot(g.astype(BF16), w_ref[...], preferred_element_type=F32) + b_ref[...]
    o_ref[...] = (g * _sigmoid(z)).astype(o_ref.dtype)


def _glu(y, u, d, w, b):
    m, n = y.shape
    row, fixed = (lambda i: (i, 0)), (lambda i: (0, 0))
    return pl.pallas_call(
        _glu_kernel,
        grid=(m // TM,),
        in_specs=[pl.BlockSpec((TM, n), row), pl.BlockSpec((TM, n), row), pl.BlockSpec((1, n), fixed),
                  pl.BlockSpec((n, n), fixed), pl.BlockSpec((1, n), fixed)],
        out_specs=pl.BlockSpec((TM, n), row),
        out_shape=jax.ShapeDtypeStruct((m, n), BF16),
        compiler_params=_params("parallel"),
        name="s5_glu",
    )(y, u, d.reshape(1, n), w, b.reshape(1, n))


def _pool_prompt_kernel(cur_ref, halo_ref, w_ref, scale_ref, o_ref, buf, *, tile, tiles_per_seq):
    i = pl.program_id(0)
    first = (i % tiles_per_seq) == 0
    buf[0:POOL_HALO, :] = jnp.where(first, 0.0, halo_ref[...])
    buf[POOL_HALO:, :] = cur_ref[...]
    pos = (i % tiles_per_seq) * tile + lax.broadcasted_iota(jnp.int32, (tile, POOL_GROUP), 0)
    for gi, win in enumerate(POOL_WINDOWS):
        cols = slice(gi * POOL_GROUP, (gi + 1) * POOL_GROUP)
        x = buf[POOL_HALO:, cols]
        acc = jnp.zeros_like(x)
        for k in range(win):
            acc = acc + buf[POOL_HALO - k:POOL_HALO - k + tile, cols]
        cnt = jnp.minimum(pos + 1, win).astype(F32)
        m = (acc / cnt - x).astype(BF16)
        y = jnp.dot(m, w_ref[gi], preferred_element_type=F32)
        o_ref[:, cols] = (y * scale_ref[:, cols]).astype(o_ref.dtype)


def _pool_prompt(u, col_block, w, scale, bsz, seq):
    tile = POOL_TILE
    tiles_per_seq = seq // tile
    halo_blocks = tile // POOL_HALO
    n = POOL_W
    return pl.pallas_call(
        functools.partial(_pool_prompt_kernel, tile=tile, tiles_per_seq=tiles_per_seq),
        grid=(bsz * tiles_per_seq,),
        in_specs=[pl.BlockSpec((tile, n), lambda i: (i, col_block)),
                  pl.BlockSpec((POOL_HALO, n), lambda i: (jnp.maximum(i * halo_blocks - 1, 0), col_block)),
                  pl.BlockSpec((len(POOL_WINDOWS), POOL_GROUP, POOL_GROUP), lambda i: (0, 0, 0)),
                  pl.BlockSpec((1, n), lambda i: (0, 0))],
        out_specs=pl.BlockSpec((tile, n), lambda i: (i, 0)),
        out_shape=jax.ShapeDtypeStruct((bsz * seq, n), BF16),
        scratch_shapes=[pltpu.VMEM((POOL_HALO + tile, n), F32)],
        compiler_params=_params("parallel"),
        name="pool_prompt",
    )(u, u, w, scale.reshape(1, n))


def _pool_sample_kernel(xc_ref, w_ref, scale_ref, o_ref, *, steps, start_pos):
    for gi, win in enumerate(POOL_WINDOWS):
        cols = slice(gi * POOL_GROUP, (gi + 1) * POOL_GROUP)
        for t in range(steps):
            x = xc_ref[POOL_HALO + t, :, cols]
            acc = jnp.zeros_like(x)
            for k in range(win):
                acc = acc + xc_ref[POOL_HALO + t - k, :, cols]
            cnt = float(min(start_pos + t + 1, win))
            m = (acc / cnt - x).astype(BF16)
            y = jnp.dot(m, w_ref[gi], preferred_element_type=F32)
            o_ref[t, :, cols] = (y * scale_ref[:, cols]).astype(o_ref.dtype)


def _pool_sample(xc, w, scale, steps, start_pos):
    rows, bsz, n = xc.shape
    whole3 = lambda i: (0, 0, 0)
    return pl.pallas_call(
        functools.partial(_pool_sample_kernel, steps=steps, start_pos=start_pos),
        grid=(1,),
        in_specs=[pl.BlockSpec((rows, bsz, n), whole3),
                  pl.BlockSpec((len(POOL_WINDOWS), POOL_GROUP, POOL_GROUP), whole3),
                  pl.BlockSpec((1, n), lambda i: (0, 0))],
        out_specs=pl.BlockSpec((steps, bsz, n), whole3),
        out_shape=jax.ShapeDtypeStruct((steps, bsz, n), BF16),
        compiler_params=_params("arbitrary"),
        name="pool_sample",
    )(xc, w, scale.reshape(1, n))


def _merge_kernel(g0_ref, g1_ref, g2_ref, a_ref, b_ref, c_ref, wa_ref, wb_ref, wc_ref, o_ref):
    dot = lambda x, w: jnp.dot(x[...], w[...], preferred_element_type=F32)
    merged = (g0_ref[...].astype(F32) * dot(a_ref, wa_ref) + g1_ref[...].astype(F32) * dot(b_ref, wb_ref)
              + g2_ref[...].astype(F32) * dot(c_ref, wc_ref))
    o_ref[...] = merged.astype(o_ref.dtype)


def _merge(gates, o_ssm, o_mla, o_pool, w_ssm, w_mla, w_pool):
    m = gates.shape[0]
    tn = 1024
    nb = D_MODEL // tn
    xspec = lambda k: pl.BlockSpec((TM, k), lambda i, j: (i, 0))
    wspec = lambda k: pl.BlockSpec((k, tn), lambda i, j: (0, j))
    gspec = lambda b: pl.BlockSpec((TM, tn), lambda i, j: (i, b * nb + j))
    return pl.pallas_call(
        _merge_kernel,
        grid=(m // TM, nb),
        in_specs=[gspec(0), gspec(1), gspec(2), xspec(SSM_W), xspec(N_HEADS * V_DIM), xspec(POOL_W),
                  wspec(SSM_W), wspec(N_HEADS * V_DIM), wspec(POOL_W)],
        out_specs=pl.BlockSpec((TM, tn), lambda i, j: (i, j)),
        out_shape=jax.ShapeDtypeStruct((m, D_MODEL), BF16),
        compiler_params=_params("parallel", "parallel"),
        name="merge",
    )(gates, gates, gates, o_ssm, o_mla, o_pool, w_ssm, w_mla, w_pool)


def _swiglu_kernel(x_ref, wg_ref, wu_ref, o_ref):
    x = x_ref[...]
    a = jnp.dot(x, wg_ref[...], preferred_element_type=F32)
    b = jnp.dot(x, wu_ref[...], preferred_element_type=F32)
    o_ref[...] = (a * _sigmoid(a) * b).astype(o_ref.dtype)


def _swiglu(h, wg, wu):
    m, k = h.shape
    n = wg.shape[1]
    tn = 512
    return pl.pallas_call(
        _swiglu_kernel,
        grid=(m // TM, n // tn),
        in_specs=[pl.BlockSpec((TM, k), lambda i, j: (i, 0)), pl.BlockSpec((k, tn), lambda i, j: (0, j)),
                  pl.BlockSpec((k, tn), lambda i, j: (0, j))],
        out_specs=pl.BlockSpec((TM, tn), lambda i, j: (i, j)),
        out_shape=jax.ShapeDtypeStruct((m, n), BF16),
        compiler_params=_params("parallel", "parallel"),
        name="swiglu",
    )(h, wg, wu)


def _rope_tables():
    inv = 1.0 / (ROPE_BASE ** (jnp.arange(0, QK_ROPE, 2, dtype=F32) / QK_ROPE))
    pos = jnp.concatenate([jnp.tile(jnp.arange(SEQ), BATCH), jnp.tile(PAST_LEN + jnp.arange(DEC_SEQ), DEC_BATCH)])
    ang = pos.astype(F32)[:, None] * inv[None, :]
    zeros = jnp.zeros((N_TOK, LANES - QK_ROPE), F32)
    cos_t = jnp.concatenate([jnp.cos(ang), jnp.cos(ang), zeros], axis=1)
    sin_t = jnp.concatenate([jnp.sin(ang), jnp.sin(ang), zeros], axis=1)
    return cos_t, sin_t


def _with_rotate_half(w):
    half = QK_ROPE // 2
    return jnp.concatenate([w, -w[:, half:], w[:, :half]], axis=1)


def _regroup(u, bsz, n_chunks, chunk):
    x = u.reshape(bsz, n_chunks, chunk, SSM_GROUPS, SSM_GROUP)
    return x.transpose(3, 0, 1, 2, 4).reshape(SSM_GROUPS, bsz * n_chunks, chunk * SSM_GROUP)


def _ungroup(y, bsz, n_chunks, chunk):
    x = y.reshape(SSM_GROUPS, bsz, n_chunks, chunk, SSM_GROUP)
    return x.transpose(1, 2, 3, 0, 4).reshape(bsz * n_chunks * chunk, SSM_W)


def _split_state(s):
    s = s.transpose(1, 0, 2)
    return s[..., :SSM_STATE], s[..., SSM_STATE:]


def kernel(x_prompt, x_sample, cache_kv_latent, cache_k_rope, state_ssm_re, state_ssm_im, state_pool_buf,
           page_table, norm_mix, norm_ffn, norm_final, w_in,
           ssm_a_re, ssm_a_im, ssm_log_dt, ssm_b_re, ssm_b_im, ssm_c_re, ssm_c_im, ssm_d, ssm_w_glu, ssm_b_glu,
           mla_q_norm, mla_w_q_b, mla_kv_norm, mla_w_kv_b, pool_w, pool_scale,
           w_br_ssm, w_br_mla, w_br_pool, w_out, ffn_w_gate, ffn_w_up, ffn_w_down):
    cos_t, sin_t = _rope_tables()
    pt_flat = page_table.reshape(-1).astype(jnp.int32)
    cache_rope_t = jnp.swapaxes(cache_k_rope, 2, 3)
    x = jnp.concatenate([x_prompt.reshape(N_PROMPT, D_MODEL), x_sample.reshape(N_SAMPLE, D_MODEL)], axis=0)
    n_chunks = SEQ // S5_CHUNK
    zero_state = jnp.zeros((SSM_GROUPS, BATCH, 2 * SSM_STATE), F32)
    outs = {k: [] for k in ("p_lat", "p_kr", "p_re", "p_im", "p_buf", "s_lat", "s_kr", "s_re", "s_im", "s_buf")}

    for l in range(DEPTH):
        wi = w_in[l]
        w_main = jnp.concatenate([wi[:, OFF_SSM:OFF_Q], wi[:, OFF_POOL:OFF_GATE]], axis=1).astype(BF16)
        w_lat = wi[:, OFF_Q:OFF_KR].astype(BF16)
        w_kr = _with_rotate_half(wi[:, OFF_KR:OFF_POOL]).astype(BF16)
        w_gates = wi[:, OFF_GATE:].astype(BF16)
        wq = mla_w_q_b[l].reshape(Q_LORA, N_HEADS, QK_NOPE + QK_ROPE)
        wq = jnp.concatenate([wq[..., :QK_NOPE], wq[..., QK_NOPE:], -wq[..., QK_NOPE + QK_ROPE // 2:],
                              wq[..., QK_NOPE:QK_NOPE + QK_ROPE // 2]], axis=-1)
        wq = wq.reshape(Q_LORA, N_HEADS * HEAD_PAD).astype(BF16)
        wkv = mla_w_kv_b[l].reshape(KV_LORA, N_HEADS, QK_NOPE + V_DIM)
        w_uk, w_uv = wkv[..., :QK_NOPE], wkv[..., QK_NOPE:]
        wk = w_uk.reshape(KV_LORA, N_HEADS * QK_NOPE).astype(BF16)
        wv = w_uv.reshape(KV_LORA, N_HEADS * V_DIM).astype(BF16)
        w_uk_t = w_uk.transpose(1, 2, 0).astype(BF16)
        w_uv_h = w_uv.transpose(1, 0, 2).astype(BF16)

        h = _rmsnorm(x, norm_mix[l], BF16)
        u_main = _linear(h, w_main, F32, tn=1024, name="in_proj_main")
        u_ssm, u_pool = u_main[:, :SSM_W], u_main[:, SSM_W:]
        gates = _linear(h, w_gates, BF16, tn=1024, act="sigmoid", name="in_proj_gates")
        lat32, lat16 = _latent_proj(h, w_lat, jnp.stack([mla_q_norm[l], mla_kv_norm[l]]).reshape(2, 1, Q_LORA))
        c_q16, c_kv32, c_kv16 = lat16[:, :Q_LORA], lat32[:, Q_LORA:], lat16[:, Q_LORA:]
        kr32, kr16 = _krope_proj(h, w_kr, cos_t, sin_t)
        q_full = _q_proj(c_q16, wq, cos_t, sin_t)
        outs["p_lat"].append(c_kv32[:N_PROMPT].reshape(BATCH, SEQ, KV_LORA))
        outs["s_lat"].append(c_kv32[N_PROMPT:].reshape(DEC_BATCH, DEC_SEQ, KV_LORA))
        outs["p_kr"].append(kr32[:N_PROMPT, :QK_ROPE].reshape(BATCH, SEQ, QK_ROPE))
        outs["s_kr"].append(kr32[N_PROMPT:, :QK_ROPE].reshape(DEC_BATCH, DEC_SEQ, QK_ROPE))

        ssm = (ssm_a_re[l], ssm_a_im[l], ssm_log_dt[l], ssm_b_re[l], ssm_b_im[l], ssm_c_re[l], ssm_c_im[l])
        u16 = u_ssm.astype(BF16)
        y_p, fin_p = _s5_core(_regroup(u16[:N_PROMPT], BATCH, n_chunks, S5_CHUNK), _s5_prep(*ssm, S5_CHUNK),
                              zero_state, BATCH, n_chunks)
        s0 = jnp.concatenate([state_ssm_re[l], state_ssm_im[l]], axis=-1).astype(F32).transpose(1, 0, 2)
        y_s, fin_s = _s5_core(_regroup(u16[N_PROMPT:], DEC_BATCH, 1, DEC_SEQ), _s5_prep(*ssm, DEC_SEQ),
                              s0, DEC_BATCH, 1)
        y_ssm = jnp.concatenate([_ungroup(y_p, BATCH, n_chunks, S5_CHUNK), _ungroup(y_s, DEC_BATCH, 1, DEC_SEQ)])
        o_ssm = _glu(y_ssm, u_main, ssm_d[l], ssm_w_glu[l].astype(BF16), ssm_b_glu[l])
        for key, fin in (("p", fin_p), ("s", fin_s)):
            re, im = _split_state(fin)
            outs[key + "_re"].append(re)
            outs[key + "_im"].append(im)

        k_full, v = _kv_proj(lat16, wk, wv, kr16, N_PROMPT)
        o_mla_p = _flash_attention(q_full, k_full, v, BATCH, SEQ)
        q_s = q_full[N_PROMPT:]
        q_lat = _headwise_linear(q_s, w_uk_t, lambda hd: 2 * hd, "q_absorb")
        q_lat = q_lat.reshape(DEC_BATCH, DEC_SEQ * N_HEADS, KV_LORA)
        q_rope = q_s.reshape(N_SAMPLE, N_HEADS, HEAD_PAD)[:, :, QK_NOPE:QK_NOPE + QK_ROPE]
        q_rope = q_rope.reshape(DEC_BATCH, DEC_SEQ * N_HEADS, QK_ROPE)
        pad_rows = ((0, 0), (0, 8 - DEC_SEQ), (0, 0))
        c_self = jnp.pad(c_kv16[N_PROMPT:].reshape(DEC_BATCH, DEC_SEQ, KV_LORA), pad_rows)
        r_self = jnp.pad(kr16[N_PROMPT:, :QK_ROPE].reshape(DEC_BATCH, DEC_SEQ, QK_ROPE), pad_rows)
        o_lat = _paged_attention(l, pt_flat, q_lat, q_rope, c_self, r_self, cache_kv_latent, cache_rope_t)
        o_mla_s = _headwise_linear(o_lat.reshape(N_SAMPLE, N_HEADS * KV_LORA), w_uv_h, lambda hd: hd, "v_absorb")
        o_mla = jnp.concatenate([o_mla_p, o_mla_s], axis=0)

        pw16 = pool_w[l].astype(BF16)
        o_pool_p = _pool_prompt(u_main, 1, pw16, pool_scale[l], BATCH, SEQ)
        u_pool_s = u_pool[N_PROMPT:].reshape(DEC_BATCH, DEC_SEQ, POOL_W)
        xc_s = jnp.concatenate([jnp.zeros((DEC_BATCH, POOL_HALO - POOL_BUF, POOL_W), F32),
                                state_pool_buf[l].astype(F32), u_pool_s], axis=1)
        o_pool_s = _pool_sample(xc_s.transpose(1, 0, 2), pw16, pool_scale[l], DEC_SEQ, PAST_LEN)
        o_pool = jnp.concatenate([o_pool_p, o_pool_s.transpose(1, 0, 2).reshape(N_SAMPLE, POOL_W)], axis=0)
        outs["p_buf"].append(u_pool[:N_PROMPT].reshape(BATCH, SEQ, POOL_W)[:, SEQ - POOL_BUF:])
        outs["s_buf"].append(xc_s[:, -POOL_BUF:])

        merged = _merge(gates, o_ssm, o_mla, o_pool, w_br_ssm[l].astype(BF16), w_br_mla[l].astype(BF16),
                        w_br_pool[l].astype(BF16))
        x = _linear(merged, w_out[l].astype(BF16), F32, tn=1024, residual=x, name="out_proj")
        h2 = _rmsnorm(x, norm_ffn[l], BF16)
        act = _swiglu(h2, ffn_w_gate[l].astype(BF16), ffn_w_up[l].astype(BF16))
        x = _linear(act, ffn_w_down[l].astype(BF16), F32, tn=1024, residual=x, name="ffn_down")

    y = _rmsnorm(x, norm_final, F32)
    stack = lambda key: jnp.stack(outs[key])
    return (y[:N_PROMPT].reshape(BATCH, SEQ, D_MODEL), y[N_PROMPT:].reshape(DEC_BATCH, DEC_SEQ, D_MODEL),
            stack("p_lat"), stack("p_kr"), stack("p_re"), stack("p_im"), stack("p_buf"),
            stack("s_lat"), stack("s_kr"), stack("s_re"), stack("s_im"), stack("s_buf"))
```

```python
import functools
import math

import jax
import jax.numpy as jnp
from jax import lax
from jax.experimental import pallas as pl
from jax.experimental.pallas import tpu as pltpu

F32 = jnp.float32
BF16 = jnp.bfloat16

D_MODEL = 2048
BATCH = 4
SEQ = 2048
DEPTH = 4
DEC_BATCH = 128
DEC_SEQ = 4
PAST_LEN = 8192
PAGE_SIZE = 128
N_PAGES = PAST_LEN // PAGE_SIZE
SSM_W = 1024
SSM_GROUP = 16
SSM_GROUPS = SSM_W // SSM_GROUP
SSM_STATE = 64
N_HEADS = 16
Q_LORA = 512
KV_LORA = 512
QK_NOPE = 128
QK_ROPE = 64
V_DIM = 128
ROPE_BASE = 10000.0
ATTN_SCALE = (QK_NOPE + QK_ROPE) ** -0.5
NEG = -1e30
POOL_W = 1024
POOL_WINDOWS = (2, 4, 8, 16)
POOL_GROUP = POOL_W // len(POOL_WINDOWS)
POOL_BUF = max(POOL_WINDOWS) - 1
POOL_HALO = 16
N_BRANCH = 3
D_FF = (((8 * D_MODEL + 2) // 3 + 255) // 256) * 256
EPS = 1e-6
OFF_SSM = 0
OFF_Q = OFF_SSM + SSM_W
OFF_KV = OFF_Q + Q_LORA
OFF_KR = OFF_KV + KV_LORA
OFF_POOL = OFF_KR + QK_ROPE
OFF_GATE = OFF_POOL + POOL_W

N_PROMPT = BATCH * SEQ
N_SAMPLE = DEC_BATCH * DEC_SEQ
N_TOK = N_PROMPT + N_SAMPLE
HEAD_PAD = 256
LANES = 128
S5_CHUNK = 16
S5_GROUPS_PER_STEP = 8
PAGES_PER_STEP = 16
POOL_TILE = 256
TM = 512
VMEM_LIMIT = 56 * 1024 * 1024


def _params(*sem):
    return pltpu.CompilerParams(dimension_semantics=sem, vmem_limit_bytes=VMEM_LIMIT)


def _sigmoid(x):
    return 0.5 * jnp.tanh(0.5 * x) + 0.5


def _rmsnorm_kernel(x_ref, g_ref, o_ref):
    x = x_ref[...]
    ms = jnp.mean(x * x, axis=-1, keepdims=True)
    o_ref[...] = (x * lax.rsqrt(ms + EPS) * g_ref[...]).astype(o_ref.dtype)


def _rmsnorm(x, g, out_dtype):
    m, d = x.shape
    return pl.pallas_call(
        _rmsnorm_kernel,
        grid=(m // TM,),
        in_specs=[pl.BlockSpec((TM, d), lambda i: (i, 0)), pl.BlockSpec((1, d), lambda i: (0, 0))],
        out_specs=pl.BlockSpec((TM, d), lambda i: (i, 0)),
        out_shape=jax.ShapeDtypeStruct((m, d), out_dtype),
        compiler_params=_params("parallel"),
        name="rmsnorm",
    )(x, g.reshape(1, d))


NT_DIMS = (((1,), (1,)), ((), ()))


def _linear_kernel(*refs, act, has_res, wt):
    x_ref, w_ref = refs[0], refs[1]
    o_ref = refs[-1]
    if wt:
        z = lax.dot_general(x_ref[...], w_ref[...], NT_DIMS, preferred_element_type=F32)
    else:
        z = jnp.dot(x_ref[...], w_ref[...], preferred_element_type=F32)
    if act == "sigmoid":
        z = _sigmoid(z)
    if has_res:
        z = refs[2][...] + z
    o_ref[...] = z.astype(o_ref.dtype)


def _layer_weight_spec(layer, k, tn):
    return pl.BlockSpec((None, k, tn), lambda j, i: (layer, 0, j))


def _linear(x, w, out_dtype, *, tn, act=None, residual=None, wt=False, layer=None, name="linear"):
    m, k = x.shape
    n = w.shape[0] if wt else w.shape[-1]
    tm = min(TM, m)
    if wt:
        w_spec = pl.BlockSpec((tn, k), lambda j, i: (j, 0))
    elif layer is not None:
        w_spec = _layer_weight_spec(layer, k, tn)
    else:
        w_spec = pl.BlockSpec((k, tn), lambda j, i: (0, j))
    in_specs = [pl.BlockSpec((tm, k), lambda j, i: (i, 0)), w_spec]
    args = [x, w]
    if residual is not None:
        in_specs.append(pl.BlockSpec((tm, tn), lambda j, i: (i, j)))
        args.append(residual)
    return pl.pallas_call(
        functools.partial(_linear_kernel, act=act, has_res=residual is not None, wt=wt),
        grid=(n // tn, m // tm),
        in_specs=in_specs,
        out_specs=pl.BlockSpec((tm, tn), lambda j, i: (i, j)),
        out_shape=jax.ShapeDtypeStruct((m, n), out_dtype),
        compiler_params=_params("parallel", "parallel"),
        name=name,
    )(*args)


def _latent_kernel(x_ref, w_ref, g_ref, o32_ref, o16_ref):
    z = lax.dot_general(x_ref[...], w_ref[...], NT_DIMS, preferred_element_type=F32)
    ms = jnp.mean(z * z, axis=-1, keepdims=True)
    y = z * lax.rsqrt(ms + EPS) * g_ref[0]
    o32_ref[...] = y
    o16_ref[...] = y.astype(BF16)


def _latent_proj(h, w, gains):
    m, k = h.shape
    n = w.shape[0]
    tn = Q_LORA
    return pl.pallas_call(
        _latent_kernel,
        grid=(m // TM, n // tn),
        in_specs=[pl.BlockSpec((TM, k), lambda i, j: (i, 0)),
                  pl.BlockSpec((tn, k), lambda i, j: (j, 0)),
                  pl.BlockSpec((1, 1, tn), lambda i, j: (j, 0, 0))],
        out_specs=[pl.BlockSpec((TM, tn), lambda i, j: (i, j)), pl.BlockSpec((TM, tn), lambda i, j: (i, j))],
        out_shape=[jax.ShapeDtypeStruct((m, n), F32), jax.ShapeDtypeStruct((m, n), BF16)],
        compiler_params=_params("parallel", "parallel"),
        name="latent_proj",
    )(h, w, gains)


def _rope_tile(z, cos_t, sin_t):
    return z * cos_t + pltpu.roll(z, 64, axis=1) * sin_t


def _krope_kernel(x_ref, w_ref, cos_ref, sin_ref, o32_ref, o16_ref):
    z = lax.dot_general(x_ref[...], w_ref[...], NT_DIMS, preferred_element_type=F32)
    y = _rope_tile(z, cos_ref[...], sin_ref[...])
    o32_ref[...] = y
    o16_ref[...] = y.astype(BF16)


def _krope_proj(h, w, cos_t, sin_t):
    m, k = h.shape
    row = lambda i: (i, 0)
    return pl.pallas_call(
        _krope_kernel,
        grid=(m // TM,),
        in_specs=[pl.BlockSpec((TM, k), row), pl.BlockSpec((LANES, k), lambda i: (0, 0)),
                  pl.BlockSpec((TM, LANES), row), pl.BlockSpec((TM, LANES), row)],
        out_specs=[pl.BlockSpec((TM, LANES), row), pl.BlockSpec((TM, LANES), row)],
        out_shape=[jax.ShapeDtypeStruct((m, LANES), F32), jax.ShapeDtypeStruct((m, LANES), BF16)],
        compiler_params=_params("parallel"),
        name="krope_proj",
    )(h, w, cos_t, sin_t)


def _q_kernel(x_ref, w_ref, cos_ref, sin_ref, o_ref, *, heads):
    z = jnp.dot(x_ref[...], w_ref[...], preferred_element_type=F32) * ATTN_SCALE
    cos_t, sin_t = cos_ref[...], sin_ref[...]
    for h in range(heads):
        base = h * HEAD_PAD
        o_ref[:, base:base + QK_NOPE] = z[:, base:base + QK_NOPE].astype(BF16)
        o_ref[:, base + QK_NOPE:base + HEAD_PAD] = _rope_tile(
            z[:, base + QK_NOPE:base + HEAD_PAD], cos_t, sin_t).astype(BF16)


def _q_proj(c_q, w, cos_t, sin_t):
    m, k = c_q.shape[0], Q_LORA
    n = w.shape[1]
    heads = KV_HEADS_PER_STEP
    tn = heads * HEAD_PAD
    return pl.pallas_call(
        functools.partial(_q_kernel, heads=heads),
        grid=(m // TM, n // tn),
        in_specs=[pl.BlockSpec((TM, k), lambda i, j: (i, 0)), pl.BlockSpec((k, tn), lambda i, j: (0, j)),
                  pl.BlockSpec((TM, LANES), lambda i, j: (i, 0)), pl.BlockSpec((TM, LANES), lambda i, j: (i, 0))],
        out_specs=pl.BlockSpec((TM, tn), lambda i, j: (i, j)),
        out_shape=jax.ShapeDtypeStruct((m, n), BF16),
        compiler_params=_params("parallel", "parallel"),
        name="q_proj",
    )(c_q, w, cos_t, sin_t)


FLASH_T = 512
KV_HEADS_PER_STEP = 8


def _kv_kernel(x_ref, wk_ref, wvt_ref, kr_ref, k_ref, vt_ref, *, heads):
    x = x_ref[...]
    zk = jnp.dot(x, wk_ref[...], preferred_element_type=F32).astype(BF16)
    vt_ref[0] = lax.dot_general(wvt_ref[...], x, (((1,), (1,)), ((), ())), preferred_element_type=F32).astype(BF16)
    kr = kr_ref[...]
    for h in range(heads):
        k_ref[:, h * HEAD_PAD:h * HEAD_PAD + QK_NOPE] = zk[:, h * QK_NOPE:(h + 1) * QK_NOPE]
        k_ref[:, h * HEAD_PAD + QK_NOPE:(h + 1) * HEAD_PAD] = kr


def _kv_proj(lat16, wk, wvt, kr16, m):
    k = KV_LORA
    heads = KV_HEADS_PER_STEP
    t = FLASH_T
    return pl.pallas_call(
        functools.partial(_kv_kernel, heads=heads),
        grid=(m // t, N_HEADS // heads),
        in_specs=[pl.BlockSpec((t, k), lambda i, j: (i, 1)),
                  pl.BlockSpec((k, heads * QK_NOPE), lambda i, j: (0, j)),
                  pl.BlockSpec((heads * V_DIM, k), lambda i, j: (j, 0)),
                  pl.BlockSpec((t, LANES), lambda i, j: (i, 0))],
        out_specs=[pl.BlockSpec((t, heads * HEAD_PAD), lambda i, j: (i, j)),
                   pl.BlockSpec((1, heads * V_DIM, t), lambda i, j: (i, j, 0))],
        out_shape=[jax.ShapeDtypeStruct((m, N_HEADS * HEAD_PAD), BF16),
                   jax.ShapeDtypeStruct((m // t, N_HEADS * V_DIM, t), BF16)],
        compiler_params=_params("parallel", "parallel"),
        name="kv_proj",
    )(lat16, wk, wvt, kr16)


FLASH_HEADS = 2


def _flash_kernel(q_ref, k_ref, vt_ref, o_ref, m_sc, l_sc, acc_sc, *, t):
    qi = pl.program_id(2)
    m_sc[...] = jnp.full_like(m_sc, NEG)
    l_sc[...] = jnp.zeros_like(l_sc)
    acc_sc[...] = jnp.zeros_like(acc_sc)

    def block(j, masked):
        start = pl.multiple_of(j * t, t)
        for h in range(FLASH_HEADS):
            q = q_ref[:, h * HEAD_PAD:(h + 1) * HEAD_PAD]
            k = k_ref[pl.ds(start, t), h * HEAD_PAD:(h + 1) * HEAD_PAD]
            vt = vt_ref[j, h * V_DIM:(h + 1) * V_DIM, :]
            s = lax.dot_general(k, q, (((1,), (1,)), ((), ())), preferred_element_type=F32)
            if masked:
                key = lax.broadcasted_iota(jnp.int32, s.shape, 0)
                qry = lax.broadcasted_iota(jnp.int32, s.shape, 1)
                s = jnp.where(key <= qry, s, NEG)
            m_prev = m_sc[h]
            m_new = jnp.maximum(m_prev, jnp.max(s, axis=0, keepdims=True))
            corr = jnp.exp(m_prev - m_new)
            p = jnp.exp(s - m_new)
            l_sc[h] = corr * l_sc[h] + jnp.sum(p, axis=0, keepdims=True)
            acc_sc[h] = corr * acc_sc[h] + jnp.dot(vt, p.astype(BF16), preferred_element_type=F32)
            m_sc[h] = m_new

    pl.loop(0, qi)(lambda j: block(j, False))
    block(qi, True)
    for h in range(FLASH_HEADS):
        o_ref[:, h * V_DIM:(h + 1) * V_DIM] = (acc_sc[h] / l_sc[h]).T.astype(o_ref.dtype)


def _flash_attention(q_full, k_full, vt, bsz, seq):
    t = FLASH_T
    nblk = seq // t
    hp = FLASH_HEADS
    return pl.pallas_call(
        functools.partial(_flash_kernel, t=t),
        grid=(bsz, N_HEADS // hp, nblk),
        in_specs=[pl.BlockSpec((t, hp * HEAD_PAD), lambda b, h, i: (b * nblk + i, h)),
                  pl.BlockSpec((seq, hp * HEAD_PAD), lambda b, h, i: (b, h)),
                  pl.BlockSpec((nblk, hp * V_DIM, t), lambda b, h, i: (b, h, 0))],
        out_specs=pl.BlockSpec((t, hp * V_DIM), lambda b, h, i: (b * nblk + i, h)),
        out_shape=jax.ShapeDtypeStruct((bsz * seq, N_HEADS * V_DIM), BF16),
        scratch_shapes=[pltpu.VMEM((hp, 1, t), F32), pltpu.VMEM((hp, 1, t), F32), pltpu.VMEM((hp, V_DIM, t), F32)],
        compiler_params=_params("parallel", "parallel", "arbitrary"),
        name="flash_attention",
    )(q_full, k_full, vt)


def _headwise_kernel(x_ref, w_ref, o_ref):
    o_ref[...] = jnp.dot(x_ref[...], w_ref[0], preferred_element_type=F32).astype(o_ref.dtype)


def _headwise_linear(x, w, x_block_of_head, name):
    m = x.shape[0]
    heads, kh, nh = w.shape
    return pl.pallas_call(
        _headwise_kernel,
        grid=(heads,),
        in_specs=[pl.BlockSpec((m, kh), lambda h: (0, x_block_of_head(h))),
                  pl.BlockSpec((1, kh, nh), lambda h: (h, 0, 0))],
        out_specs=pl.BlockSpec((m, nh), lambda h: (0, h)),
        out_shape=jax.ShapeDtypeStruct((m, heads * nh), BF16),
        compiler_params=_params("parallel"),
        name=name,
    )(x, w)


def _paged_kernel(pt_ref, qlat_ref, qrope_ref, cself_ref, rself_ref, lat_hbm, rope_hbm, o_ref,
                  m_sc, l_sc, acc_sc, lat_buf, rope_buf, lat16, rope16, sems, *, layer, pages, chunks_per_seq):
    c = pl.program_id(0)
    j = c % chunks_per_seq
    slot = c % 2
    nt = (((1,), (1,)), ((), ()))

    def page_copies(chunk, buf_slot):
        copies = []
        for p in range(pages):
            page = pt_ref[chunk * pages + p]
            copies.append(pltpu.make_async_copy(
                lat_hbm.at[layer, page], lat_buf.at[buf_slot, pl.ds(p * PAGE_SIZE, PAGE_SIZE)],
                sems.at[0, buf_slot]))
            copies.append(pltpu.make_async_copy(rope_hbm.at[layer, page], rope_buf.at[buf_slot, p],
                                                sems.at[1, buf_slot]))
        return copies

    def start_all(copies):
        for i, cp in enumerate(copies):
            cp.start(priority=(i // 2) % 2)

    @pl.when(c == 0)
    def _():
        start_all(page_copies(0, 0))

    @pl.when(c + 1 < pl.num_programs(0))
    def _():
        start_all(page_copies(c + 1, 1 - slot))

    @pl.when(j == 0)
    def _():
        m_sc[...] = jnp.full_like(m_sc, NEG)
        l_sc[...] = jnp.zeros_like(l_sc)
        acc_sc[...] = jnp.zeros_like(acc_sc)

    for cp in page_copies(c, slot):
        cp.wait()
    for p in range(pages):
        rows = pl.ds(p * PAGE_SIZE, PAGE_SIZE)
        lat16[rows, :] = lat_buf[slot, rows, :].astype(BF16)
        rope16[:, p * PAGE_SIZE:(p + 1) * PAGE_SIZE] = rope_buf[slot, p].astype(BF16)

    qlat, qrope = qlat_ref[0], qrope_ref[0]

    def accumulate(s, values):
        m_prev = m_sc[...]
        m_new = jnp.maximum(m_prev, jnp.max(s, axis=-1, keepdims=True))
        corr = jnp.exp(m_prev - m_new)
        p = jnp.exp(s - m_new)
        l_sc[...] = corr * l_sc[...] + jnp.sum(p, axis=-1, keepdims=True)
        acc_sc[...] = corr * acc_sc[...] + jnp.dot(p.astype(BF16), values, preferred_element_type=F32)
        m_sc[...] = m_new

    lat = lat16[...]
    s = (lax.dot_general(qlat, lat, nt, preferred_element_type=F32)
         + jnp.dot(qrope, rope16[...], preferred_element_type=F32))
    accumulate(s, lat)

    @pl.when(j == chunks_per_seq - 1)
    def _():
        cself = cself_ref[0]
        s_self = (lax.dot_general(qlat, cself, nt, preferred_element_type=F32)
                  + lax.dot_general(qrope, rself_ref[0], nt, preferred_element_type=F32))
        qpos = lax.broadcasted_iota(jnp.int32, s_self.shape, 0) // N_HEADS
        kpos = lax.broadcasted_iota(jnp.int32, s_self.shape, 1)
        accumulate(jnp.where(kpos <= qpos, s_self, NEG), cself)
        o_ref[0] = (acc_sc[...] / l_sc[...]).astype(o_ref.dtype)


def _paged_attention(layer, page_table_flat, q_lat, q_rope, c_self, r_self, cache_lat, cache_rope_t):
    pages = PAGES_PER_STEP
    chunks_per_seq = N_PAGES // pages
    rows = DEC_SEQ * N_HEADS
    self_rows = c_self.shape[1]
    per_b = lambda c, pt: (c // chunks_per_seq, 0, 0)
    in_specs = [pl.BlockSpec((1, rows, KV_LORA), per_b), pl.BlockSpec((1, rows, QK_ROPE), per_b),
                pl.BlockSpec((1, self_rows, KV_LORA), per_b), pl.BlockSpec((1, self_rows, QK_ROPE), per_b),
                pl.BlockSpec(memory_space=pl.ANY), pl.BlockSpec(memory_space=pl.ANY)]
    return pl.pallas_call(
        functools.partial(_paged_kernel, layer=layer, pages=pages, chunks_per_seq=chunks_per_seq),
        grid_spec=pltpu.PrefetchScalarGridSpec(
            num_scalar_prefetch=1,
            grid=(DEC_BATCH * chunks_per_seq,),
            in_specs=in_specs,
            out_specs=pl.BlockSpec((1, rows, KV_LORA), per_b),
            scratch_shapes=[pltpu.VMEM((rows, 1), F32), pltpu.VMEM((rows, 1), F32),
                            pltpu.VMEM((rows, KV_LORA), F32),
                            pltpu.VMEM((2, pages * PAGE_SIZE, KV_LORA), F32),
                            pltpu.VMEM((2, pages, QK_ROPE, PAGE_SIZE), F32),
                            pltpu.VMEM((pages * PAGE_SIZE, KV_LORA), BF16),
                            pltpu.VMEM((QK_ROPE, pages * PAGE_SIZE), BF16),
                            pltpu.SemaphoreType.DMA((2, 2))]),
        out_shape=jax.ShapeDtypeStruct((DEC_BATCH, rows, KV_LORA), BF16),
        compiler_params=_params("arbitrary"),
        name="paged_attention",
    )(page_table_flat, q_lat, q_rope, c_self, r_self, cache_lat, cache_rope_t)


def _cexp(k, dt, a_re, a_im):
    mag = jnp.exp(k * dt * a_re)
    ang = k * dt * a_im
    return mag * jnp.cos(ang), mag * jnp.sin(ang)


def _cmul(x_re, x_im, y_re, y_im):
    return x_re * y_re - x_im * y_im, x_re * y_im + x_im * y_re


S5_TABLE_ROWS = 16


def _s5_prep_kernel(a_re, a_im, log_dt_ref, bt_re, bt_im, ct_re, ct_im,
                    toep_ref, bst_re_ref, bst_im_ref, cst_re_ref, cst_im_ref, pw_re_ref, pw_im_ref, *, chunk):
    groups = a_re.shape[0]
    ti = chunk * SSM_GROUP
    hi = lax.Precision.HIGHEST
    nt = (((1,), (1,)), ((), ()))
    expand = (lax.broadcasted_iota(jnp.int32, (ti, S5_TABLE_ROWS), 0) // SSM_GROUP
              == lax.broadcasted_iota(jnp.int32, (ti, S5_TABLE_ROWS), 1)).astype(F32)
    rep = lambda x: jnp.dot(expand, x, precision=hi, preferred_element_type=F32)
    step = lax.broadcasted_iota(jnp.int32, (S5_TABLE_ROWS, SSM_STATE), 0).astype(F32)
    s_idx = lax.broadcasted_iota(jnp.int32, (ti, ti), 0) // SSM_GROUP
    t_idx = lax.broadcasted_iota(jnp.int32, (ti, ti), 1) // SSM_GROUP
    k_pow = chunk * jnp.left_shift(1, lax.broadcasted_iota(jnp.int32, (8, SSM_STATE), 0)).astype(F32)

    def one_group(g):
        ar, ai, dt = a_re[g], a_im[g], jnp.exp(log_dt_ref[g])
        ab_re, ab_im = _cexp(1.0, dt, ar, ai)
        den = ar * ar + ai * ai
        n_re, n_im = ab_re - 1.0, ab_im
        c_re = (n_re * ar + n_im * ai) / den
        c_im = (n_im * ar - n_re * ai) / den
        bb_re, bb_im = _cmul(c_re, c_im, bt_re[g], bt_im[g])
        em_re, em_im = _cexp(-step, dt, ar, ai)
        ep_re, ep_im = _cexp(step, dt, ar, ai)
        p_re, p_im = _cmul(rep(em_re), rep(em_im), bb_re, bb_im)
        h_re, h_im = _cmul(*_cexp(chunk - 1.0, dt, ar, ai), p_re, p_im)
        bst_re_ref[g] = h_re
        bst_im_ref[g] = h_im
        q_re, q_im = _cmul(rep(ep_re), rep(ep_im), ct_re[g], ct_im[g])
        r_re, r_im = _cmul(ab_re, ab_im, q_re, q_im)
        cst_re_ref[g] = r_re
        cst_im_ref[g] = -r_im
        toep = (lax.dot_general(p_re, q_re, nt, preferred_element_type=F32, precision=hi)
                - lax.dot_general(p_im, q_im, nt, preferred_element_type=F32, precision=hi))
        toep_ref[g] = jnp.where(t_idx >= s_idx, toep, 0.0).astype(BF16)
        pw_re, pw_im = _cexp(k_pow, dt, ar, ai)
        pw_re_ref[g] = pw_re
        pw_im_ref[g] = pw_im

    @pl.loop(0, groups // 2)
    def _(pair):
        one_group(2 * pair)
        one_group(2 * pair + 1)


def _s5_prep(a_re, a_im, log_dt, b_re, b_im, c_re, c_im, chunk):
    g, p, ti = SSM_GROUPS, SSM_STATE, chunk * SSM_GROUP
    assert chunk <= S5_TABLE_ROWS
    gs = S5_GROUPS_PER_STEP
    vec = lambda x: x.astype(F32).reshape(g, 1, p)
    bt = lambda x: jnp.tile(jnp.swapaxes(x.astype(F32), 1, 2), (1, chunk, 1))
    ct = lambda x: jnp.tile(x.astype(F32), (1, chunk, 1))
    args = [vec(a_re), vec(a_im), jnp.broadcast_to(log_dt.astype(F32).reshape(g, 1, 1), (g, 1, p)), bt(b_re), bt(b_im),
            ct(c_re), ct(c_im)]
    blk = lambda shape: pl.BlockSpec((gs,) + shape, lambda i: (i, 0, 0))
    in_shapes = [(1, p), (1, p), (1, p), (ti, p), (ti, p), (ti, p), (ti, p)]
    out_shapes = [(ti, ti), (ti, p), (ti, p), (ti, p), (ti, p), (8, p), (8, p)]
    out_dtypes = [BF16, F32, F32, F32, F32, F32, F32]
    toep, bst_re, bst_im, cst_re, cst_im, pw_re, pw_im = pl.pallas_call(
        functools.partial(_s5_prep_kernel, chunk=chunk),
        grid=(g // gs,),
        in_specs=[blk(s) for s in in_shapes],
        out_specs=[blk(s) for s in out_shapes],
        out_shape=[jax.ShapeDtypeStruct((g,) + s, d) for s, d in zip(out_shapes, out_dtypes)],
        compiler_params=_params("parallel"),
        name="s5_prep",
    )(*args)
    bst = jnp.concatenate([bst_re, bst_im], axis=-1).astype(BF16)
    cst = jnp.swapaxes(jnp.concatenate([cst_re, cst_im], axis=-1), 1, 2).astype(BF16)
    pwa = jnp.concatenate([pw_re, pw_re], axis=-1)
    pwb = jnp.concatenate([-pw_im, pw_im], axis=-1)
    return dict(toep=toep, bst=bst, cst=cst, pwa=pwa, pwb=pwb,
                bst_re=bst_re.astype(BF16), bst_im=bst_im.astype(BF16), pw_re=pw_re, pw_im=pw_im)


def _s5_step_kernel(a_ref, toep_ref, bst_ref, cst_ref, pwa_ref, pwb_ref, s0_ref, y_ref, sfin_ref):
    @pl.loop(0, a_ref.shape[0])
    def _(g):
        a = a_ref[g]
        s0 = s0_ref[g]
        s_loc = jnp.dot(a, bst_ref[g], preferred_element_type=F32)
        sfin_ref[g] = s0 * pwa_ref[g, 0:1, :] + pltpu.roll(s0, 64, axis=1) * pwb_ref[g, 0:1, :] + s_loc
        y_ref[g] = (jnp.dot(a, toep_ref[g], preferred_element_type=F32)
                    + jnp.dot(s0.astype(BF16), cst_ref[g], preferred_element_type=F32))


def _s5_step(a, ops, s0):
    toep, bst, cst, pwa, pwb = (ops[k] for k in ("toep", "bst", "cst", "pwa", "pwb"))
    g, rows, ti = a.shape
    gs = S5_GROUPS_PER_STEP
    p2 = 2 * SSM_STATE
    blk = lambda shape: pl.BlockSpec((gs,) + shape, lambda i: (i, 0, 0))
    return pl.pallas_call(
        _s5_step_kernel,
        grid=(g // gs,),
        in_specs=[blk((rows, ti)), blk((ti, ti)), blk((ti, p2)), blk((p2, ti)), blk((8, p2)), blk((8, p2)),
                  blk((rows, p2))],
        out_specs=[blk((rows, ti)), blk((rows, p2))],
        out_shape=[jax.ShapeDtypeStruct((g, rows, ti), F32), jax.ShapeDtypeStruct((g, rows, p2), F32)],
        compiler_params=_params("parallel"),
        name="s5_step",
    )(a, toep, bst, cst, pwa, pwb, s0)


S5_SCAN_PAD = 64


def _s5_seq_kernel(u_ref, toep_ref, bre_ref, bim_ref, cst_ref, pre_ref, pim_ref, y_ref, fre_ref, fim_ref,
                   a_sc, yg_sc, z_sc, buf_re, buf_im, *, chunk, n_chunks):
    groups = toep_ref.shape[0]
    w = SSM_GROUP
    pad = S5_SCAN_PAD
    p = SSM_STATE
    rows = pl.ds(pad, n_chunks)
    for t in range(chunk):
        ut = u_ref[pl.ds(t, n_chunks, stride=chunk), :]
        for g in range(groups):
            a_sc[g, :, t * w:(t + 1) * w] = ut[:, g * w:(g + 1) * w]
    buf_re[0:pad, :] = jnp.zeros((pad, p), F32)
    buf_im[0:pad, :] = jnp.zeros((pad, p), F32)

    @pl.loop(0, groups)
    def _(g):
        a = a_sc[g].astype(BF16)
        xr = jnp.dot(a, bre_ref[g], preferred_element_type=F32)
        xi = jnp.dot(a, bim_ref[g], preferred_element_type=F32)
        shift, k = 1, 0
        while shift < n_chunks:
            buf_re[rows, :] = xr
            buf_im[rows, :] = xi
            sr = buf_re[pl.ds(pad - shift, n_chunks), :]
            si = buf_im[pl.ds(pad - shift, n_chunks), :]
            wr, wi = pre_ref[g, k:k + 1, :], pim_ref[g, k:k + 1, :]
            xr, xi = xr + sr * wr - si * wi, xi + sr * wi + si * wr
            shift, k = shift * 2, k + 1
        buf_re[rows, :] = xr
        buf_im[rows, :] = xi
        prev_re = buf_re[pl.ds(pad - 1, n_chunks), :]
        prev_im = buf_im[pl.ds(pad - 1, n_chunks), :]
        fre_ref[0, g] = xr[n_chunks - 1:n_chunks, :]
        fim_ref[0, g] = xi[n_chunks - 1:n_chunks, :]
        yg_sc[g] = (jnp.dot(a, toep_ref[g], preferred_element_type=F32)
                    + jnp.dot(prev_re.astype(BF16), cst_ref[g, 0:p, :], preferred_element_type=F32)
                    + jnp.dot(prev_im.astype(BF16), cst_ref[g, p:2 * p, :], preferred_element_type=F32))

    for t in range(chunk):
        for g in range(groups):
            z_sc[:, g * w:(g + 1) * w] = yg_sc[g, :, t * w:(t + 1) * w]
        y_ref[pl.ds(t, n_chunks, stride=chunk), :] = z_sc[...]


def _s5_sequences(u, ops, bsz, seq, chunk):
    n_chunks = seq // chunk
    ti = chunk * SSM_GROUP
    p = SSM_STATE
    lanes = LANES
    gs = lanes // SSM_GROUP
    assert n_chunks <= 2 * S5_SCAN_PAD
    blk = lambda shape: pl.BlockSpec((gs,) + shape, lambda b, j: (j, 0, 0))
    fin_spec = pl.BlockSpec((1, gs, 1, p), lambda b, j: (b, j, 0, 0))
    fin_shape = jax.ShapeDtypeStruct((bsz, SSM_GROUPS, 1, p), F32)
    return pl.pallas_call(
        functools.partial(_s5_seq_kernel, chunk=chunk, n_chunks=n_chunks),
        grid=(bsz, SSM_W // lanes),
        in_specs=[pl.BlockSpec((seq, lanes), lambda b, j: (b, j)),
                  blk((ti, ti)), blk((ti, p)), blk((ti, p)), blk((2 * p, ti)), blk((8, p)), blk((8, p))],
        out_specs=[pl.BlockSpec((seq, lanes), lambda b, j: (b, j)), fin_spec, fin_spec],
        out_shape=[jax.ShapeDtypeStruct((bsz * seq, SSM_W), F32), fin_shape, fin_shape],
        scratch_shapes=[pltpu.VMEM((gs, n_chunks, ti), F32), pltpu.VMEM((gs, n_chunks, ti), F32),
                        pltpu.VMEM((n_chunks, lanes), F32),
                        pltpu.VMEM((S5_SCAN_PAD + n_chunks, p), F32), pltpu.VMEM((S5_SCAN_PAD + n_chunks, p), F32)],
        compiler_params=_params("parallel", "arbitrary"),
        name="s5_sequences",
    )(u, ops["toep"], ops["bst_re"], ops["bst_im"], ops["cst"], ops["pw_re"], ops["pw_im"])


def _two_part_specs(tm, width, prompt_tiles, row_axis):
    return (pl.BlockSpec((tm, width), lambda *idx: (jnp.minimum(idx[row_axis], prompt_tiles - 1), 0)),
            pl.BlockSpec((tm, width), lambda *idx: (jnp.maximum(idx[row_axis] - prompt_tiles, 0), 0)))


def _two_part_load(p_ref, s_ref, prompt_tiles, row_axis):
    return jnp.where(pl.program_id(row_axis) >= prompt_tiles, s_ref[...], p_ref[...])


def _glu_kernel(yp_ref, ys_ref, u_ref, d_ref, w_ref, b_ref, o_ref, *, prompt_tiles):
    x = _two_part_load(yp_ref, ys_ref, prompt_tiles, 0) + d_ref[...] * u_ref[...]
    g = 0.5 * x * (1.0 + jnp.tanh(math.sqrt(2.0 / math.pi) * (x + 0.044715 * (x * x * x))))
    z = jnp.dot(g.astype(BF16), w_ref[...], preferred_element_type=F32) + b_ref[...]
    o_ref[...] = (g * _sigmoid(z)).astype(o_ref.dtype)


def _glu(y_p, y_s, u, d, w, b, layer):
    n = y_p.shape[1]
    m = y_p.shape[0] + y_s.shape[0]
    prompt_tiles = y_p.shape[0] // TM
    row, fixed = (lambda i: (i, 0)), (lambda i: (0, 0))
    return pl.pallas_call(
        functools.partial(_glu_kernel, prompt_tiles=prompt_tiles),
        grid=(m // TM,),
        in_specs=[*_two_part_specs(TM, n, prompt_tiles, 0), pl.BlockSpec((TM, n), row), pl.BlockSpec((1, n), fixed),
                  pl.BlockSpec((None, n, n), lambda i: (layer, 0, 0)), pl.BlockSpec((1, n), fixed)],
        out_specs=pl.BlockSpec((TM, n), row),
        out_shape=jax.ShapeDtypeStruct((m, n), BF16),
        compiler_params=_params("parallel"),
        name="s5_glu",
    )(y_p, y_s, u, d.reshape(1, n), w, b.reshape(1, n))


def _pool_prompt_kernel(cur_ref, halo_ref, w_ref, scale_ref, o_ref, buf, *, tile, tiles_per_seq):
    i = pl.program_id(0)
    first = (i % tiles_per_seq) == 0
    buf[0:POOL_HALO, :] = jnp.where(first, 0.0, halo_ref[...])
    buf[POOL_HALO:, :] = cur_ref[...]
    pos = (i % tiles_per_seq) * tile + lax.broadcasted_iota(jnp.int32, (tile, POOL_GROUP), 0)
    for gi, win in enumerate(POOL_WINDOWS):
        cols = slice(gi * POOL_GROUP, (gi + 1) * POOL_GROUP)
        x = buf[POOL_HALO:, cols]
        acc = jnp.zeros_like(x)
        for k in range(win):
            acc = acc + buf[POOL_HALO - k:POOL_HALO - k + tile, cols]
        cnt = jnp.minimum(pos + 1, win).astype(F32)
        m = (acc / cnt - x).astype(BF16)
        y = jnp.dot(m, w_ref[gi], preferred_element_type=F32)
        o_ref[:, cols] = (y * scale_ref[:, cols]).astype(o_ref.dtype)


def _pool_prompt(u, col_block, w, scale, bsz, seq):
    tile = POOL_TILE
    tiles_per_seq = seq // tile
    halo_blocks = tile // POOL_HALO
    n = POOL_W
    return pl.pallas_call(
        functools.partial(_pool_prompt_kernel, tile=tile, tiles_per_seq=tiles_per_seq),
        grid=(bsz * tiles_per_seq,),
        in_specs=[pl.BlockSpec((tile, n), lambda i: (i, col_block)),
                  pl.BlockSpec((POOL_HALO, n), lambda i: (jnp.maximum(i * halo_blocks - 1, 0), col_block)),
                  pl.BlockSpec((len(POOL_WINDOWS), POOL_GROUP, POOL_GROUP), lambda i: (0, 0, 0)),
                  pl.BlockSpec((1, n), lambda i: (0, 0))],
        out_specs=pl.BlockSpec((tile, n), lambda i: (i, 0)),
        out_shape=jax.ShapeDtypeStruct((bsz * seq, n), BF16),
        scratch_shapes=[pltpu.VMEM((POOL_HALO + tile, n), F32)],
        compiler_params=_params("parallel"),
        name="pool_prompt",
    )(u, u, w, scale.reshape(1, n))


def _pool_sample_kernel(xc_ref, w_ref, scale_ref, o_ref, *, steps, start_pos):
    for gi, win in enumerate(POOL_WINDOWS):
        cols = slice(gi * POOL_GROUP, (gi + 1) * POOL_GROUP)
        for t in range(steps):
            x = xc_ref[POOL_HALO + t, :, cols]
            acc = jnp.zeros_like(x)
            for k in range(win):
                acc = acc + xc_ref[POOL_HALO + t - k, :, cols]
            cnt = float(min(start_pos + t + 1, win))
            m = (acc / cnt - x).astype(BF16)
            y = jnp.dot(m, w_ref[gi], preferred_element_type=F32)
            o_ref[t, :, cols] = (y * scale_ref[:, cols]).astype(o_ref.dtype)


def _pool_sample(xc, w, scale, steps, start_pos):
    rows, bsz, n = xc.shape
    whole3 = lambda i: (0, 0, 0)
    return pl.pallas_call(
        functools.partial(_pool_sample_kernel, steps=steps, start_pos=start_pos),
        grid=(1,),
        in_specs=[pl.BlockSpec((rows, bsz, n), whole3),
                  pl.BlockSpec((len(POOL_WINDOWS), POOL_GROUP, POOL_GROUP), whole3),
                  pl.BlockSpec((1, n), lambda i: (0, 0))],
        out_specs=pl.BlockSpec((steps, bsz, n), whole3),
        out_shape=jax.ShapeDtypeStruct((steps, bsz, n), BF16),
        compiler_params=_params("arbitrary"),
        name="pool_sample",
    )(xc, w, scale.reshape(1, n))


def _merge_kernel(g0_ref, g1_ref, g2_ref, a_ref, bp_ref, bs_ref, cp_ref, cs_ref, wa_ref, wb_ref, wc_ref, o_ref,
                  *, prompt_tiles):
    dot = lambda x, w: jnp.dot(x, w[...], preferred_element_type=F32)
    o_mla = _two_part_load(bp_ref, bs_ref, prompt_tiles, 1)
    o_pool = _two_part_load(cp_ref, cs_ref, prompt_tiles, 1)
    merged = (g0_ref[...].astype(F32) * dot(a_ref[...], wa_ref) + g1_ref[...].astype(F32) * dot(o_mla, wb_ref)
              + g2_ref[...].astype(F32) * dot(o_pool, wc_ref))
    o_ref[...] = merged.astype(o_ref.dtype)


def _merge(gates, o_ssm, o_mla, o_pool, w_ssm, w_mla, w_pool, layer):
    m = gates.shape[0]
    tn = 1024
    nb = D_MODEL // tn
    prompt_tiles = o_mla[0].shape[0] // TM
    xspec = lambda k: pl.BlockSpec((TM, k), lambda j, i: (i, 0))
    wspec = lambda k: _layer_weight_spec(layer, k, tn)
    gspec = lambda b: pl.BlockSpec((TM, tn), lambda j, i: (i, b * nb + j))
    return pl.pallas_call(
        functools.partial(_merge_kernel, prompt_tiles=prompt_tiles),
        grid=(nb, m // TM),
        in_specs=[gspec(0), gspec(1), gspec(2), xspec(SSM_W),
                  *_two_part_specs(TM, N_HEADS * V_DIM, prompt_tiles, 1),
                  *_two_part_specs(TM, POOL_W, prompt_tiles, 1),
                  wspec(SSM_W), wspec(N_HEADS * V_DIM), wspec(POOL_W)],
        out_specs=pl.BlockSpec((TM, tn), lambda j, i: (i, j)),
        out_shape=jax.ShapeDtypeStruct((m, D_MODEL), BF16),
        compiler_params=_params("parallel", "parallel"),
        name="merge",
    )(gates, gates, gates, o_ssm, *o_mla, *o_pool, w_ssm, w_mla, w_pool)


def _swiglu_kernel(x_ref, wg_ref, wu_ref, o_ref):
    x = x_ref[...]
    a = jnp.dot(x, wg_ref[...], preferred_element_type=F32)
    b = jnp.dot(x, wu_ref[...], preferred_element_type=F32)
    o_ref[...] = (a * _sigmoid(a) * b).astype(o_ref.dtype)


def _swiglu(h, wg, wu, layer):
    m, k = h.shape
    n = wg.shape[-1]
    tn = 512
    return pl.pallas_call(
        _swiglu_kernel,
        grid=(n // tn, m // TM),
        in_specs=[pl.BlockSpec((TM, k), lambda j, i: (i, 0)), _layer_weight_spec(layer, k, tn),
                  _layer_weight_spec(layer, k, tn)],
        out_specs=pl.BlockSpec((TM, tn), lambda j, i: (i, j)),
        out_shape=jax.ShapeDtypeStruct((m, n), BF16),
        compiler_params=_params("parallel", "parallel"),
        name="swiglu",
    )(h, wg, wu)


def _rope_tables():
    inv = 1.0 / (ROPE_BASE ** (jnp.arange(0, QK_ROPE, 2, dtype=F32) / QK_ROPE))
    pos = jnp.concatenate([jnp.tile(jnp.arange(SEQ), BATCH), jnp.tile(PAST_LEN + jnp.arange(DEC_SEQ), DEC_BATCH)])
    ang = pos.astype(F32)[:, None] * inv[None, :]
    zeros = jnp.zeros((N_TOK, LANES - QK_ROPE), F32)
    cos_t = jnp.concatenate([jnp.cos(ang), jnp.cos(ang), zeros], axis=1)
    sin_t = jnp.concatenate([jnp.sin(ang), jnp.sin(ang), zeros], axis=1)
    return cos_t, sin_t


def _with_rotate_half(w_t):
    half = QK_ROPE // 2
    return jnp.concatenate([w_t, -w_t[half:], w_t[:half]], axis=0)


def _regroup(u, bsz, n_chunks, chunk):
    x = u.reshape(bsz, n_chunks, chunk, SSM_GROUPS, SSM_GROUP)
    return x.transpose(3, 0, 1, 2, 4).reshape(SSM_GROUPS, bsz * n_chunks, chunk * SSM_GROUP)


def _ungroup(y, bsz, n_chunks, chunk):
    x = y.reshape(SSM_GROUPS, bsz, n_chunks, chunk, SSM_GROUP)
    return x.transpose(1, 2, 3, 0, 4).reshape(bsz * n_chunks * chunk, SSM_W)


def _split_state(s):
    s = s.transpose(1, 0, 2)
    return s[..., :SSM_STATE], s[..., SSM_STATE:]


def kernel(x_prompt, x_sample, cache_kv_latent, cache_k_rope, state_ssm_re, state_ssm_im, state_pool_buf,
           page_table, norm_mix, norm_ffn, norm_final, w_in,
           ssm_a_re, ssm_a_im, ssm_log_dt, ssm_b_re, ssm_b_im, ssm_c_re, ssm_c_im, ssm_d, ssm_w_glu, ssm_b_glu,
           mla_q_norm, mla_w_q_b, mla_kv_norm, mla_w_kv_b, pool_w, pool_scale,
           w_br_ssm, w_br_mla, w_br_pool, w_out, ffn_w_gate, ffn_w_up, ffn_w_down):
    cos_t, sin_t = _rope_tables()
    pt_flat = page_table.reshape(-1).astype(jnp.int32)
    cache_rope_t = jnp.swapaxes(cache_k_rope, 2, 3)
    w_in_t = jnp.swapaxes(w_in, 1, 2).astype(BF16)
    x = jnp.concatenate([x_prompt.reshape(N_PROMPT, D_MODEL), x_sample.reshape(N_SAMPLE, D_MODEL)], axis=0)
    n_chunks = SEQ // S5_CHUNK
    zero_state = jnp.zeros((SSM_GROUPS, BATCH, 2 * SSM_STATE), F32)
    w16 = {name: w.astype(BF16) for name, w in (
        ("glu", ssm_w_glu), ("br_ssm", w_br_ssm), ("br_mla", w_br_mla), ("br_pool", w_br_pool), ("out", w_out),
        ("ffn_gate", ffn_w_gate), ("ffn_up", ffn_w_up), ("ffn_down", ffn_w_down))}
    outs = {k: [] for k in ("p_lat", "p_kr", "p_re", "p_im", "p_buf", "s_lat", "s_kr", "s_re", "s_im", "s_buf")}

    for l in range(DEPTH):
        wi = w_in_t[l]
        w_main = jnp.concatenate([wi[OFF_SSM:OFF_Q], wi[OFF_POOL:OFF_GATE]], axis=0)
        w_lat = wi[OFF_Q:OFF_KR]
        w_kr = _with_rotate_half(wi[OFF_KR:OFF_POOL])
        w_gates = wi[OFF_GATE:]
        wq = mla_w_q_b[l].reshape(Q_LORA, N_HEADS, QK_NOPE + QK_ROPE)
        wq = jnp.concatenate([wq[..., :QK_NOPE], wq[..., QK_NOPE:], -wq[..., QK_NOPE + QK_ROPE // 2:],
                              wq[..., QK_NOPE:QK_NOPE + QK_ROPE // 2]], axis=-1)
        wq = wq.reshape(Q_LORA, N_HEADS * HEAD_PAD).astype(BF16)
        wkv = mla_w_kv_b[l].reshape(KV_LORA, N_HEADS, QK_NOPE + V_DIM)
        w_uk, w_uv = wkv[..., :QK_NOPE], wkv[..., QK_NOPE:]
        wk = w_uk.reshape(KV_LORA, N_HEADS * QK_NOPE).astype(BF16)
        wvt = w_uv.transpose(1, 2, 0).reshape(N_HEADS * V_DIM, KV_LORA).astype(BF16)
        w_uk_t = w_uk.transpose(1, 2, 0).astype(BF16)
        w_uv_h = w_uv.transpose(1, 0, 2).astype(BF16)

        h = _rmsnorm(x, norm_mix[l], BF16)
        u_main = _linear(h, w_main, F32, tn=1024, wt=True, name="in_proj_main")
        u_ssm, u_pool = u_main[:, :SSM_W], u_main[:, SSM_W:]
        gates = _linear(h, w_gates, BF16, tn=1024, act="sigmoid", wt=True, name="in_proj_gates")
        lat32, lat16 = _latent_proj(h, w_lat, jnp.stack([mla_q_norm[l], mla_kv_norm[l]]).reshape(2, 1, Q_LORA))
        c_q16, c_kv32, c_kv16 = lat16[:, :Q_LORA], lat32[:, Q_LORA:], lat16[:, Q_LORA:]
        kr32, kr16 = _krope_proj(h, w_kr, cos_t, sin_t)
        q_full = _q_proj(lat16, wq, cos_t, sin_t)
        outs["p_lat"].append(c_kv32[:N_PROMPT].reshape(BATCH, SEQ, KV_LORA))
        outs["s_lat"].append(c_kv32[N_PROMPT:].reshape(DEC_BATCH, DEC_SEQ, KV_LORA))
        outs["p_kr"].append(kr32[:N_PROMPT, :QK_ROPE].reshape(BATCH, SEQ, QK_ROPE))
        outs["s_kr"].append(kr32[N_PROMPT:, :QK_ROPE].reshape(DEC_BATCH, DEC_SEQ, QK_ROPE))

        ssm = (ssm_a_re[l], ssm_a_im[l], ssm_log_dt[l], ssm_b_re[l], ssm_b_im[l], ssm_c_re[l], ssm_c_im[l])
        y_p, fin_re, fin_im = _s5_sequences(u_main, _s5_prep(*ssm, S5_CHUNK), BATCH, SEQ, S5_CHUNK)
        s0 = jnp.concatenate([state_ssm_re[l], state_ssm_im[l]], axis=-1).astype(F32).transpose(1, 0, 2)
        y_s, fin_s = _s5_step(_regroup(u_ssm[N_PROMPT:].astype(BF16), DEC_BATCH, 1, DEC_SEQ),
                              _s5_prep(*ssm, DEC_SEQ), s0)
        o_ssm = _glu(y_p, _ungroup(y_s, DEC_BATCH, 1, DEC_SEQ), u_main, ssm_d[l], w16["glu"], ssm_b_glu[l], l)
        outs["p_re"].append(fin_re[:, :, 0, :])
        outs["p_im"].append(fin_im[:, :, 0, :])
        re, im = _split_state(fin_s)
        outs["s_re"].append(re)
        outs["s_im"].append(im)

        k_full, vt = _kv_proj(lat16, wk, wvt, kr16, N_PROMPT)
        o_mla_p = _flash_attention(q_full, k_full, vt, BATCH, SEQ)
        q_s = q_full[N_PROMPT:]
        q_lat = _headwise_linear(q_s, w_uk_t, lambda hd: 2 * hd, "q_absorb")
        q_lat = q_lat.reshape(DEC_BATCH, DEC_SEQ * N_HEADS, KV_LORA)
        q_rope = q_s.reshape(N_SAMPLE, N_HEADS, HEAD_PAD)[:, :, QK_NOPE:QK_NOPE + QK_ROPE]
        q_rope = q_rope.reshape(DEC_BATCH, DEC_SEQ * N_HEADS, QK_ROPE)
        pad_rows = ((0, 0), (0, 8 - DEC_SEQ), (0, 0))
        c_self = jnp.pad(c_kv16[N_PROMPT:].reshape(DEC_BATCH, DEC_SEQ, KV_LORA), pad_rows)
        r_self = jnp.pad(kr16[N_PROMPT:, :QK_ROPE].reshape(DEC_BATCH, DEC_SEQ, QK_ROPE), pad_rows)
        o_lat = _paged_attention(l, pt_flat, q_lat, q_rope, c_self, r_self, cache_kv_latent, cache_rope_t)
        o_mla_s = _headwise_linear(o_lat.reshape(N_SAMPLE, N_HEADS * KV_LORA), w_uv_h, lambda hd: hd, "v_absorb")
        o_mla = (o_mla_p, o_mla_s)

        pw16 = pool_w[l].astype(BF16)
        o_pool_p = _pool_prompt(u_main, 1, pw16, pool_scale[l], BATCH, SEQ)
        u_pool_s = u_pool[N_PROMPT:].reshape(DEC_BATCH, DEC_SEQ, POOL_W)
        xc_s = jnp.concatenate([jnp.zeros((DEC_BATCH, POOL_HALO - POOL_BUF, POOL_W), F32),
                                state_pool_buf[l].astype(F32), u_pool_s], axis=1)
        o_pool_s = _pool_sample(xc_s.transpose(1, 0, 2), pw16, pool_scale[l], DEC_SEQ, PAST_LEN)
        o_pool = (o_pool_p, o_pool_s.transpose(1, 0, 2).reshape(N_SAMPLE, POOL_W))
        outs["p_buf"].append(jnp.stack([u_main[(b + 1) * SEQ - POOL_BUF:(b + 1) * SEQ, SSM_W:] for b in range(BATCH)]))
        outs["s_buf"].append(xc_s[:, -POOL_BUF:])

        merged = _merge(gates, o_ssm, o_mla, o_pool, w16["br_ssm"], w16["br_mla"], w16["br_pool"], l)
        x = _linear(merged, w16["out"], F32, tn=1024, residual=x, layer=l, name="out_proj")
        h2 = _rmsnorm(x, norm_ffn[l], BF16)
        act = _swiglu(h2, w16["ffn_gate"], w16["ffn_up"], l)
        x = _linear(act, w16["ffn_down"], F32, tn=1024, residual=x, layer=l, name="ffn_down")

    y = _rmsnorm(x, norm_final, F32)
    stack = lambda key: jnp.stack(outs[key])
    return (y[:N_PROMPT].reshape(BATCH, SEQ, D_MODEL), y[N_PROMPT:].reshape(DEC_BATCH, DEC_SEQ, D_MODEL),
            stack("p_lat"), stack("p_kr"), stack("p_re"), stack("p_im"), stack("p_buf"),
            stack("s_lat"), stack("s_kr"), stack("s_re"), stack("s_im"), stack("s_buf"))
```
